```python
import math
import jax, jax.numpy as jnp
from jax import lax
import numpy as np

D_MODEL = 4096
BATCH = 1
SEQ = 16384
DEPTH = 2

CTX_LEN = 256
GRID_W = 64
N_MOD = 6
MLP_HIDDEN = 4 * D_MODEL
ATTN_HEADS = 16
ATTN_QK_DIM = 64
ATTN_V_DIM = 2 * ATTN_QK_DIM
ATTN_W = ATTN_HEADS * ATTN_V_DIM
Q_W = ATTN_HEADS * 2 * ATTN_QK_DIM
K_W = ATTN_HEADS * 2 * ATTN_QK_DIM
V_W = ATTN_W
POOL_W = D_MODEL // 4
POOL_WINDOWS = (2, 4, 8, 16)
POOL_GROUP = POOL_W // len(POOL_WINDOWS)
CONV_W = D_MODEL // 4
CONV_K = 31
KV_OFF = Q_W
KV_END = Q_W + K_W + V_W
POOL_OFF = KV_END
CONV_OFF = POOL_OFF + POOL_W
IN_W = CONV_OFF + 2 * CONV_W
MIX_W = ATTN_W + POOL_W + CONV_W
Q_BLOCK = 128
ROPE_BASE = 10000.0
EPS = 1e-6

kernel_name = "hymba_style_diffattn_pool_conv_dit"


def rms_norm(x, g):
    xf = x.astype(jnp.float32)
    y = xf * lax.rsqrt(jnp.mean(xf * xf, axis=-1, keepdims=True) + EPS)
    return (y * g.astype(jnp.float32)).astype(x.dtype)


def layer_norm(x, g, b):
    xf = x.astype(jnp.float32)
    mu = jnp.mean(xf, axis=-1, keepdims=True)
    var = jnp.mean(jnp.square(xf - mu), axis=-1, keepdims=True)
    y = (xf - mu) * lax.rsqrt(var + EPS)
    return (y * g.astype(jnp.float32) + b.astype(jnp.float32)).astype(x.dtype)


def modulation(cond, w_ada, b_ada):
    m = jax.nn.silu(cond) @ w_ada + b_ada
    return jnp.split(m, N_MOD, axis=-1)


def modulate(x, g, shift, scale):
    return rms_norm(x, g) * (1.0 + scale[:, None, :]) + shift[:, None, :]


def axial_rope_tables(n_tok):
    rows = n_tok // GRID_W
    r = jnp.repeat(jnp.arange(rows), GRID_W).astype(jnp.float32)
    col = jnp.tile(jnp.arange(GRID_W), rows).astype(jnp.float32)
    n_freq = ATTN_QK_DIM // 4
    inv = ROPE_BASE ** (-jnp.arange(n_freq, dtype=jnp.float32) / n_freq)
    ang = jnp.concatenate([r[:, None] * inv, col[:, None] * inv], axis=-1)
    return jnp.cos(ang), jnp.sin(ang)


def apply_rope(t, cos, sin):
    half = t.shape[-1] // 2
    cs = cos[None, :, None, None, :]
    sn = sin[None, :, None, None, :]
    t1 = t[..., :half].astype(jnp.float32)
    t2 = t[..., half:].astype(jnp.float32)
    out = jnp.concatenate([t1 * cs - t2 * sn, t2 * cs + t1 * sn], axis=-1)
    return out.astype(t.dtype)


def q_heads(u_q, qn):
    b, L, _ = u_q.shape
    return rms_norm(u_q.reshape(b, L, ATTN_HEADS, 2, ATTN_QK_DIM), qn)


def kv_heads(u_kv, kn):
    b, L, _ = u_kv.shape
    k = rms_norm(u_kv[..., :K_W].reshape(b, L, ATTN_HEADS, 2, ATTN_QK_DIM), kn)
    v = u_kv[..., K_W:].reshape(b, L, ATTN_HEADS, ATTN_V_DIM)
    return k, v


def diff_attn_block(q, k, v, lam):
    scale = ATTN_QK_DIM ** -0.5
    s1 = jnp.einsum('bqhd,bkhd->bhqk', q[..., 0, :], k[..., 0, :]).astype(jnp.float32) * scale
    s2 = jnp.einsum('bqhd,bkhd->bhqk', q[..., 1, :], k[..., 1, :]).astype(jnp.float32) * scale
    w = jax.nn.softmax(s1, axis=-1) - lam * jax.nn.softmax(s2, axis=-1)
    return jnp.einsum('bhqk,bkhe->bqhe', w.astype(v.dtype), v)


def latent_diff_attention(q, k_all, v_all, lam):
    b, s = q.shape[:2]
    nb = s // Q_BLOCK
    qb = jnp.moveaxis(q.reshape(b, nb, Q_BLOCK, ATTN_HEADS, 2, ATTN_QK_DIM), 1, 0)
    out = lax.map(lambda qq: diff_attn_block(qq, k_all, v_all, lam), qb)
    return jnp.moveaxis(out, 0, 1).reshape(b, s, ATTN_HEADS, ATTN_V_DIM)


def pool_mixer(u, w_grp, scale):
    b, L, _ = u.shape
    uf = u.astype(jnp.float32)
    csum = jnp.concatenate([jnp.zeros((b, 1, POOL_W), jnp.float32), jnp.cumsum(uf, axis=1)], axis=1)
    t = jnp.arange(L)
    outs = []
    for g, w in enumerate(POOL_WINDOWS):
        lo = jnp.clip(t - w // 2, 0, L)
        hi = jnp.clip(t + w - w // 2, 0, L)
        sl = slice(g * POOL_GROUP, (g + 1) * POOL_GROUP)
        cnt = (hi - lo).astype(jnp.float32)[None, :, None]
        outs.append((csum[:, hi, sl] - csum[:, lo, sl]) / cnt - uf[:, :, sl])
    pooled = jnp.stack(outs, axis=2).astype(u.dtype)
    y = jnp.einsum('blgc,gcd->blgd', pooled, w_grp).reshape(b, L, POOL_W)
    return y * scale


def conv_module(u, w_dw, b_dw, ln_g, ln_b, w_pw):
    a, gt = jnp.split(u, 2, axis=-1)
    g = a * jax.nn.sigmoid(gt)
    y = lax.conv_general_dilated(
        g, w_dw[:, None, :], window_strides=(1,), padding=[(CONV_K // 2, CONV_K // 2)],
        dimension_numbers=('NWC', 'WIO', 'NWC'), feature_group_count=CONV_W) + b_dw
    y = jax.nn.silu(layer_norm(y, ln_g, ln_b))
    return y @ w_pw


def mixer_out(attn, u_pool, u_conv, lam_init, subln_g, pool_w, pool_scale,
              conv_dw, conv_db, conv_ln_g, conv_ln_b, conv_pw, w_out):
    b, L = attn.shape[:2]
    a = (rms_norm(attn, subln_g) * (1.0 - lam_init)).reshape(b, L, ATTN_W)
    p = pool_mixer(u_pool, pool_w, pool_scale)
    cv = conv_module(u_conv, conv_dw, conv_db, conv_ln_g, conv_ln_b, conv_pw)
    return jnp.concatenate([a, p, cv], axis=-1) @ w_out


def sq_relu_mlp(h, w1, w2):
    return jnp.square(jax.nn.relu(h @ w1)) @ w2


def setup_inputs(seed: int = 0) -> dict:
    key = jax.random.key(seed)
    ks = jax.random.split(key, 26)
    D = D_MODEL

    def nrm(k, shape, s):
        return jax.random.normal(k, shape, jnp.float32) * s

    return {
        "x": nrm(ks[0], (BATCH, SEQ, D), 1.0),
        "c": nrm(ks[1], (BATCH, D), 1.0),
        "ctx": nrm(ks[2], (BATCH, CTX_LEN, D), 1.0),
        "c_ctx": nrm(ks[3], (D,), 1.0),
        "w_ada": nrm(ks[4], (DEPTH, D, N_MOD * D), 0.5 * D ** -0.5),
        "b_ada": nrm(ks[5], (DEPTH, N_MOD * D), 0.02),
        "norm1": 1.0 + nrm(ks[6], (DEPTH, D), 0.05),
        "norm2": 1.0 + nrm(ks[7], (DEPTH, D), 0.05),
        "w_in": nrm(ks[8], (DEPTH, D, IN_W), D ** -0.5),
        "w_out": nrm(ks[9], (DEPTH, MIX_W, D), MIX_W ** -0.5),
        "q_norm": 1.0 + nrm(ks[10], (DEPTH, ATTN_QK_DIM), 0.05),
        "k_norm": 1.0 + nrm(ks[11], (DEPTH, ATTN_QK_DIM), 0.05),
        "lambda_q1": nrm(ks[12], (DEPTH, ATTN_QK_DIM), 0.1),
        "lambda_k1": nrm(ks[13], (DEPTH, ATTN_QK_DIM), 0.1),
        "lambda_q2": nrm(ks[14], (DEPTH, ATTN_QK_DIM), 0.1),
        "lambda_k2": nrm(ks[15], (DEPTH, ATTN_QK_DIM), 0.1),
        "subln": 1.0 + nrm(ks[16], (DEPTH, ATTN_V_DIM), 0.05),
        "pool_w": nrm(ks[17], (DEPTH, len(POOL_WINDOWS), POOL_GROUP, POOL_GROUP), POOL_GROUP ** -0.5),
        "pool_scale": 1.0 + nrm(ks[18], (DEPTH, POOL_W), 0.1),
        "conv_dw": nrm(ks[19], (DEPTH, CONV_K, CONV_W), CONV_K ** -0.5),
        "conv_db": nrm(ks[20], (DEPTH, CONV_W), 0.02),
        "conv_ln_g": 1.0 + nrm(ks[21], (DEPTH, CONV_W), 0.05),
        "conv_ln_b": nrm(ks[22], (DEPTH, CONV_W), 0.02),
        "conv_pw": nrm(ks[23], (DEPTH, CONV_W, CONV_W), CONV_W ** -0.5),
        "w_mlp1": nrm(ks[24], (DEPTH, D, MLP_HIDDEN), D ** -0.5),
        "w_mlp2": nrm(ks[25], (DEPTH, MLP_HIDDEN, D), MLP_HIDDEN ** -0.5),
    }


def reference(x, c, ctx, c_ctx, w_ada, b_ada, norm1, norm2, w_in, w_out, q_norm, k_norm,
              lambda_q1, lambda_k1, lambda_q2, lambda_k2, subln, pool_w, pool_scale,
              conv_dw, conv_db, conv_ln_g, conv_ln_b, conv_pw, w_mlp1, w_mlp2):
    n_tok = x.shape[1]
    cos, sin = axial_rope_tables(n_tok)
    xc = ctx.astype(x.dtype)
    for l in range(DEPTH):
        last = l == DEPTH - 1
        lam_init = 0.8 - 0.6 * math.exp(-0.3 * l)
        lam = (jnp.exp(jnp.sum(lambda_q1[l].astype(jnp.float32) * lambda_k1[l].astype(jnp.float32)))
               - jnp.exp(jnp.sum(lambda_q2[l].astype(jnp.float32) * lambda_k2[l].astype(jnp.float32)))
               + lam_init)
        sh1, sc1, g1, sh2, sc2, g2 = modulation(c, w_ada[l], b_ada[l])
        csh1, csc1, cg1, csh2, csc2, cg2 = modulation(c_ctx[None, :], w_ada[l], b_ada[l])

        h = modulate(x, norm1[l], sh1, sc1)
        hc = modulate(xc, norm1[l], csh1, csc1)
        u = h @ w_in[l]
        w_ctx = w_in[l][:, KV_OFF:KV_END] if last else w_in[l]
        uc = hc @ w_ctx
        kv_start = 0 if last else KV_OFF
        kc, vc = kv_heads(uc[..., kv_start:kv_start + K_W + V_W], k_norm[l])

        q = apply_rope(q_heads(u[..., :Q_W], q_norm[l]), cos, sin)
        k, v = kv_heads(u[..., KV_OFF:KV_END], k_norm[l])
        k = apply_rope(k, cos, sin)
        k_all = jnp.concatenate([kc, k], axis=1)
        v_all = jnp.concatenate([vc, v], axis=1)
        attn = latent_diff_attention(q, k_all, v_all, lam)
        y = mixer_out(attn, u[..., POOL_OFF:CONV_OFF], u[..., CONV_OFF:], lam_init, subln[l],
                      pool_w[l], pool_scale[l], conv_dw[l], conv_db[l], conv_ln_g[l], conv_ln_b[l],
                      conv_pw[l], w_out[l])
        x_new = x + g1[:, None, :] * y

        x_new = x_new + g2[:, None, :] * sq_relu_mlp(modulate(x_new, norm2[l], sh2, sc2), w_mlp1[l], w_mlp2[l])

        if not last:
            qc = q_heads(uc[..., :Q_W], q_norm[l])
            attn_c = diff_attn_block(qc, kc, vc, lam)
            yc = mixer_out(attn_c, uc[..., POOL_OFF:CONV_OFF], uc[..., CONV_OFF:], lam_init, subln[l],
                           pool_w[l], pool_scale[l], conv_dw[l], conv_db[l], conv_ln_g[l], conv_ln_b[l],
                           conv_pw[l], w_out[l])
            xc = xc + cg1[:, None, :] * yc
            xc = xc + cg2[:, None, :] * sq_relu_mlp(modulate(xc, norm2[l], csh2, csc2), w_mlp1[l], w_mlp2[l])
        x = x_new
    return x
```

```python
import functools
import math

import jax
import jax.numpy as jnp
from jax import lax
from jax.experimental import pallas as pl
from jax.experimental.pallas import tpu as pltpu

F32 = jnp.float32
BF16 = jnp.bfloat16

ATTN_HEADS = 16
QK_DIM = 64
HEAD_W = 2 * QK_DIM
GRID_W = 64
N_MOD = 6
POOL_WINDOWS = (2, 4, 8, 16)
CONV_K = 31
HALO = 16
ROPE_BASE = 10000.0
EPS = 1e-6

LANES = 128
MXU_W = 256
VMEM_BYTES = 64 << 20


def _params(sem, vmem_mb):
    assert (vmem_mb << 20) <= VMEM_BYTES
    return pltpu.CompilerParams(dimension_semantics=sem, vmem_limit_bytes=vmem_mb << 20)


def _blk(n, pref):
    if n <= pref:
        return n
    b = pref
    while n % b:
        b //= 2
    return b


def _ada_kernel(cond_ref, w_ref, b_ref, o_ref):
    s = jax.nn.silu(cond_ref[...]).astype(BF16)
    o_ref[...] = jnp.dot(s, w_ref[...].astype(BF16), preferred_element_type=F32) + b_ref[...]


def _ada(cond8, w_ada, b_ada):
    depth, d, n = w_ada.shape
    bn = _blk(n, 512)
    return pl.pallas_call(
        _ada_kernel,
        grid=(depth, n // bn),
        in_specs=[
            pl.BlockSpec((8, d), lambda l, j: (0, 0)),
            pl.BlockSpec((None, d, bn), lambda l, j: (l, 0, j)),
            pl.BlockSpec((None, 1, bn), lambda l, j: (l, 0, j)),
        ],
        out_specs=pl.BlockSpec((None, 8, bn), lambda l, j: (l, 0, j)),
        out_shape=jax.ShapeDtypeStruct((depth, 8, n), F32),
        compiler_params=_params(("arbitrary", "arbitrary"), 40),
        name="ada_mod",
    )(cond8, w_ada, b_ada.reshape(depth, 1, n))


def _prenorm_kernel(x_ref, g_ref, sc_ref, sh_ref, o_ref):
    x = x_ref[...]
    y = x * lax.rsqrt(jnp.mean(x * x, axis=-1, keepdims=True) + EPS) * g_ref[...]
    o_ref[...] = (y * (1.0 + sc_ref[...]) + sh_ref[...]).astype(o_ref.dtype)


def _prenorm(x, g, scale, shift):
    L, d = x.shape
    bm = _blk(L, 512)
    row = pl.BlockSpec((1, d), lambda i: (0, 0))
    return pl.pallas_call(
        _prenorm_kernel,
        grid=(L // bm,),
        in_specs=[pl.BlockSpec((bm, d), lambda i: (i, 0)), row, row, row],
        out_specs=pl.BlockSpec((bm, d), lambda i: (i, 0)),
        out_shape=jax.ShapeDtypeStruct((L, d), BF16),
        compiler_params=_params(("arbitrary",), 40),
        name="prenorm",
    )(x, g.reshape(1, d), scale.reshape(1, d), shift.reshape(1, d))


def _mm_cast_kernel(a_ref, w_ref, o_ref, *, relu2):
    acc = jnp.dot(a_ref[...], w_ref[...], preferred_element_type=F32)
    if relu2:
        acc = jnp.square(jnp.maximum(acc, 0.0))
    o_ref[...] = acc.astype(o_ref.dtype)


def _mm_cast(a, w, col_off, n, *, relu2=False, bm=1024, bn=1024):
    M, K = a.shape
    bm, bn = _blk(M, bm), _blk(n, bn)
    assert col_off % bn == 0
    joff = col_off // bn
    return pl.pallas_call(
        functools.partial(_mm_cast_kernel, relu2=relu2),
        grid=(M // bm, n // bn),
        in_specs=[
            pl.BlockSpec((bm, K), lambda i, j: (i, 0)),
            pl.BlockSpec((K, bn), lambda i, j: (0, j + joff)),
        ],
        out_specs=pl.BlockSpec((bm, bn), lambda i, j: (i, j)),
        out_shape=jax.ShapeDtypeStruct((M, n), BF16),
        compiler_params=_params(("arbitrary", "arbitrary"), 48),
        name="proj_relu2" if relu2 else "proj_cast",
    )(a, w)


def _mm_vt_kernel(a_ref, w_ref, o_ref, *, tk):
    acc = jnp.dot(a_ref[...], w_ref[...], preferred_element_type=F32)
    for c in range(a_ref.shape[0] // tk):
        o_ref[c] = acc[c * tk:(c + 1) * tk, :].T.astype(o_ref.dtype)


def _mm_vt(a, w, col_off, n, tk, *, bm=1024, bn=512):
    M, K = a.shape
    bm, bn = _blk(M, bm), _blk(n, bn)
    assert bm % tk == 0 and col_off % bn == 0
    joff = col_off // bn
    return pl.pallas_call(
        functools.partial(_mm_vt_kernel, tk=tk),
        grid=(M // bm, n // bn),
        in_specs=[
            pl.BlockSpec((bm, K), lambda i, j: (i, 0)),
            pl.BlockSpec((K, bn), lambda i, j: (0, j + joff)),
        ],
        out_specs=pl.BlockSpec((bm // tk, bn, tk), lambda i, j: (i, j, 0)),
        out_shape=jax.ShapeDtypeStruct((M // tk, n, tk), BF16),
        compiler_params=_params(("arbitrary", "arbitrary"), 48),
        name="proj_vt",
    )(a, w)


def _mm_qk_kernel(a_ref, w_ref, gain_ref, gmat_ref, *rest, rope):
    if rope:
        cos_ref, sin_ref, o_ref = rest
    else:
        (o_ref,) = rest
    acc = jnp.dot(a_ref[...], w_ref[...], preferred_element_type=F32)
    bm, bn = acc.shape
    if rope:
        cos, sin = cos_ref[...], sin_ref[...]
        low_half = (lax.broadcasted_iota(jnp.int32, (bm, LANES), 1) & (QK_DIM // 2)) == 0
    for c in range(bn // MXU_W):
        t = acc[:, c * MXU_W:(c + 1) * MXU_W]
        ss = jnp.dot((t * t).astype(BF16), gmat_ref[...], preferred_element_type=F32)
        y = t * lax.rsqrt(ss * (1.0 / QK_DIM) + EPS) * gain_ref[:, c * MXU_W:(c + 1) * MXU_W]
        for h in range(MXU_W // LANES):
            yy = y[:, h * LANES:(h + 1) * LANES]
            if rope:
                partner = jnp.where(low_half, pltpu.roll(yy, LANES - QK_DIM // 2, 1), pltpu.roll(yy, QK_DIM // 2, 1))
                yy = yy * cos + partner * sin
            lo = c * MXU_W + h * LANES
            o_ref[:, lo:lo + LANES] = yy.astype(o_ref.dtype)


def _mm_qk(a, w, gain_row, gmat, cos=None, sin=None, *, bm=1024, bn=1024):
    M, K = a.shape
    n = gain_row.shape[1]
    bm, bn = _blk(M, bm), _blk(n, bn)
    rope = cos is not None
    in_specs = [
        pl.BlockSpec((bm, K), lambda i, j: (i, 0)),
        pl.BlockSpec((K, bn), lambda i, j: (0, j)),
        pl.BlockSpec((1, bn), lambda i, j: (0, j)),
        pl.BlockSpec((MXU_W, MXU_W), lambda i, j: (0, 0)),
    ]
    args = [a, w, gain_row, gmat]
    if rope:
        in_specs += [pl.BlockSpec((bm, LANES), lambda i, j: (i, 0))] * 2
        args += [cos, sin]
    return pl.pallas_call(
        functools.partial(_mm_qk_kernel, rope=rope),
        grid=(M // bm, n // bn),
        in_specs=in_specs,
        out_specs=pl.BlockSpec((bm, bn), lambda i, j: (i, j)),
        out_shape=jax.ShapeDtypeStruct((M, n), BF16),
        compiler_params=_params(("arbitrary", "arbitrary"), 52),
        name="proj_qk",
    )(*args)


def _mm_resid_kernel(*refs, n_lhs):
    a_refs, w_refs = refs[:n_lhs], refs[n_lhs:2 * n_lhs]
    x_ref, g_ref, o_ref, acc_ref = refs[2 * n_lhs:]
    k = pl.program_id(2)
    part = jnp.dot(a_refs[0][...], w_refs[0][...], preferred_element_type=F32)
    for a_ref, w_ref in zip(a_refs[1:], w_refs[1:]):
        part += jnp.dot(a_ref[...], w_ref[...], preferred_element_type=F32)

    @pl.when(k == 0)
    def _():
        acc_ref[...] = part

    @pl.when(k > 0)
    def _():
        acc_ref[...] += part

    @pl.when(k == pl.num_programs(2) - 1)
    def _():
        o_ref[...] = x_ref[...] + g_ref[...] * acc_ref[...]


def _mm_resid(lhs, w, x, gate, *, bk, bm=1024, bn=512):
    M, n = x.shape
    kw = lhs[0].shape[1]
    assert all(a.shape == (M, kw) for a in lhs) and w.shape == (kw * len(lhs), n) and kw % bk == 0
    bm, bn = _blk(M, bm), _blk(n, bn)
    ksteps = kw // bk
    in_specs = [pl.BlockSpec((bm, bk), lambda i, j, k: (i, k)) for _ in lhs]
    in_specs += [pl.BlockSpec((bk, bn), functools.partial(lambda i, j, k, s: (s * ksteps + k, j), s=s))
                 for s in range(len(lhs))]
    in_specs += [pl.BlockSpec((bm, bn), lambda i, j, k: (i, j)), pl.BlockSpec((1, bn), lambda i, j, k: (0, j))]
    return pl.pallas_call(
        functools.partial(_mm_resid_kernel, n_lhs=len(lhs)),
        grid=(M // bm, n // bn, ksteps),
        in_specs=in_specs,
        out_specs=pl.BlockSpec((bm, bn), lambda i, j, k: (i, j)),
        out_shape=jax.ShapeDtypeStruct((M, n), F32),
        scratch_shapes=[pltpu.VMEM((bm, bn), F32)],
        compiler_params=_params(("arbitrary", "arbitrary", "arbitrary"), 48),
        name="proj_resid",
    )(*lhs, *([w] * len(lhs)), x, gate.reshape(1, n))


def _attn_kernel(*refs, tq, tk, n_lat, lam_init):
    if n_lat:
        lamp_ref, subln_ref, q_ref, kc_ref, vc_ref, k_ref, v_ref, o_ref, qz_ref, m_ref, l_ref, acc_ref = refs
    else:
        lamp_ref, subln_ref, q_ref, kc_ref, vc_ref, o_ref, qz_ref, m_ref, l_ref, acc_ref = refs

    q = q_ref[...]
    lane = lax.broadcasted_iota(jnp.int32, q.shape, 1)
    qz_ref[0:tq, :] = jnp.where(lane < QK_DIM, q, jnp.zeros_like(q))
    qz_ref[tq:2 * tq, :] = jnp.where(lane >= QK_DIM, q, jnp.zeros_like(q))
    m_ref[...] = jnp.full(m_ref.shape, -jnp.inf, F32)
    l_ref[...] = jnp.zeros(l_ref.shape, F32)
    acc_ref[...] = jnp.zeros(acc_ref.shape, F32)

    def step(k_blk, vt_blk):
        s = lax.dot_general(k_blk, qz_ref[...], (((1,), (1,)), ((), ())), preferred_element_type=F32)
        m_old = m_ref[...]
        m_new = jnp.maximum(m_old, jnp.max(s, axis=0, keepdims=True))
        alpha = jnp.exp(m_old - m_new)
        p = jnp.exp(s - m_new)
        l_ref[...] = alpha * l_ref[...] + jnp.sum(p, axis=0, keepdims=True)
        acc_ref[...] = alpha * acc_ref[...] + jnp.dot(vt_blk, p.astype(BF16), preferred_element_type=F32)
        m_ref[...] = m_new

    for c in range(kc_ref.shape[0] // tk):
        step(kc_ref[c * tk:(c + 1) * tk, :], vc_ref[c])

    if n_lat:
        def body(c, carry):
            step(k_ref[pl.ds(pl.multiple_of(c * tk, tk), tk), :], v_ref[c])
            return carry
        lax.fori_loop(0, n_lat, body, 0)

    lp = lamp_ref[...]
    lam = (jnp.exp(jnp.sum(lp[0:1] * lp[1:2], axis=1, keepdims=True))
           - jnp.exp(jnp.sum(lp[2:3] * lp[3:4], axis=1, keepdims=True)) + lam_init)
    o = acc_ref[...] / l_ref[...]
    d = o[:, :tq] - lam * o[:, tq:]
    d = d * lax.rsqrt(jnp.mean(d * d, axis=0, keepdims=True) + EPS)
    o_ref[...] = (d.T * subln_ref[...] * (1.0 - lam_init)).astype(o_ref.dtype)


def _attention(lamp, subln, q_arr, q_col0, kc_arr, kc_col0, vct, k_arr=None, k_col0=0, vt=None, *, lam_init, tq=512):
    Lq = q_arr.shape[0]
    C = kc_arr.shape[0]
    n_ctx_chunks, hw, tk = vct.shape
    heads = hw // HEAD_W
    tq = _blk(Lq, tq)
    n_lat = 0 if k_arr is None else k_arr.shape[0] // tk
    in_specs = [
        pl.BlockSpec((4, QK_DIM), lambda h, i: (0, 0)),
        pl.BlockSpec((1, HEAD_W), lambda h, i: (0, 0)),
        pl.BlockSpec((tq, HEAD_W), lambda h, i: (i, q_col0 + h)),
        pl.BlockSpec((C, HEAD_W), lambda h, i: (0, kc_col0 + h)),
        pl.BlockSpec((n_ctx_chunks, HEAD_W, tk), lambda h, i: (0, h, 0)),
    ]
    args = [lamp, subln.reshape(1, HEAD_W), q_arr, kc_arr, vct]
    if n_lat:
        in_specs += [
            pl.BlockSpec((k_arr.shape[0], HEAD_W), lambda h, i: (0, k_col0 + h)),
            pl.BlockSpec((n_lat, HEAD_W, tk), lambda h, i: (0, h, 0)),
        ]
        args += [k_arr, vt]
    return pl.pallas_call(
        functools.partial(_attn_kernel, tq=tq, tk=tk, n_lat=n_lat, lam_init=lam_init),
        grid=(heads, Lq // tq),
        in_specs=in_specs,
        out_specs=pl.BlockSpec((tq, HEAD_W), lambda h, i: (i, h)),
        out_shape=jax.ShapeDtypeStruct((Lq, hw), BF16),
        scratch_shapes=[
            pltpu.VMEM((2 * tq, HEAD_W), BF16),
            pltpu.VMEM((1, 2 * tq), F32),
            pltpu.VMEM((1, 2 * tq), F32),
            pltpu.VMEM((HEAD_W, 2 * tq), F32),
        ],
        compiler_params=_params(("arbitrary", "arbitrary"), 48),
        name="diff_attn",
    )(*args)


def _mixer_kernel(u_ref, up_ref, un_ref, pw_ref, ps_ref, dw_ref, db_ref, lg_ref, lb_ref, cw_ref, o_ref,
                  pool_ref, glu_ref, *, seq_len):
    bm = u_ref.shape[0]
    pw_cols = pool_ref.shape[1]
    grp = pw_cols // len(POOL_WINDOWS)
    i = pl.program_id(0)
    has_prev = (i > 0).astype(F32)
    has_next = (i < pl.num_programs(0) - 1).astype(F32)

    def fill(row0, src, valid):
        u = src.astype(F32)
        pool_ref[row0:row0 + src.shape[0], :] = u[:, :pw_cols] * valid
        a, gate = u[:, pw_cols:2 * pw_cols], u[:, 2 * pw_cols:]
        glu_ref[row0:row0 + src.shape[0], :] = a * jax.nn.sigmoid(gate) * valid

    fill(0, up_ref[...], has_prev)
    fill(HALO, u_ref[...], 1.0)
    fill(HALO + bm, un_ref[...], has_next)

    t = i * bm + lax.broadcasted_iota(jnp.int32, (bm, 1), 0)
    for g, w in enumerate(POOL_WINDOWS):
        cols = slice(g * grp, (g + 1) * grp)
        tot = pool_ref[HALO - w // 2:HALO - w // 2 + bm, cols]
        for dlt in range(-w // 2 + 1, w - w // 2):
            tot = tot + pool_ref[HALO + dlt:HALO + dlt + bm, cols]
        cnt = (jnp.clip(t + (w - w // 2), 0, seq_len) - jnp.clip(t - w // 2, 0, seq_len)).astype(F32)
        pooled = tot / cnt - pool_ref[HALO:HALO + bm, cols]
        y = jnp.dot(pooled.astype(BF16), pw_ref[g], preferred_element_type=F32) * ps_ref[:, cols]
        o_ref[:, cols] = y.astype(o_ref.dtype)

    half = CONV_K // 2
    y = glu_ref[HALO - half:HALO - half + bm, :] * dw_ref[0:1, :]
    for k in range(1, CONV_K):
        y = y + glu_ref[HALO - half + k:HALO - half + k + bm, :] * dw_ref[k:k + 1, :]
    y = y + db_ref[...]
    mu = jnp.mean(y, axis=-1, keepdims=True)
    yc = y - mu
    var = jnp.mean(yc * yc, axis=-1, keepdims=True)
    z = yc * lax.rsqrt(var + EPS) * lg_ref[...] + lb_ref[...]
    z = jax.nn.silu(z)
    o_ref[:, pw_cols:] = jnp.dot(z.astype(BF16), cw_ref[...], preferred_element_type=F32).astype(o_ref.dtype)


def _mixer(upc, pool_w, pool_scale, conv_dw, conv_db, ln_g, ln_b, conv_pw, *, bm=256):
    L, w3 = upc.shape
    cw = w3 // 3
    bm = _blk(L, bm)
    nb, hb = L // bm, bm // HALO
    last_halo = L // HALO - 1
    dw_pad = jnp.zeros((32, cw), F32).at[:CONV_K].set(conv_dw)
    row = pl.BlockSpec((1, cw), lambda i: (0, 0))
    return pl.pallas_call(
        functools.partial(_mixer_kernel, seq_len=L),
        grid=(nb,),
        in_specs=[
            pl.BlockSpec((bm, w3), lambda i: (i, 0)),
            pl.BlockSpec((HALO, w3), lambda i: (jnp.maximum(i * hb - 1, 0), 0)),
            pl.BlockSpec((HALO, w3), lambda i: (jnp.minimum((i + 1) * hb, last_halo), 0)),
            pl.BlockSpec(pool_w.shape, lambda i: (0, 0, 0)),
            row,
            pl.BlockSpec((32, cw), lambda i: (0, 0)),
            row, row, row,
            pl.BlockSpec((cw, cw), lambda i: (0, 0)),
        ],
        out_specs=pl.BlockSpec((bm, 2 * cw), lambda i: (i, 0)),
        out_shape=jax.ShapeDtypeStruct((L, 2 * cw), BF16),
        scratch_shapes=[pltpu.VMEM((bm + 2 * HALO, cw), F32), pltpu.VMEM((bm + 2 * HALO, cw), F32)],
        compiler_params=_params(("arbitrary",), 40),
        name="pool_conv_mixer",
    )(upc, upc, upc, pool_w.astype(BF16), pool_scale.reshape(1, cw), dw_pad, conv_db.reshape(1, cw),
      ln_g.reshape(1, cw), ln_b.reshape(1, cw), conv_pw.astype(BF16))


def _rope_tables(n_tok):
    r = jnp.repeat(jnp.arange(n_tok // GRID_W), GRID_W).astype(F32)
    col = jnp.tile(jnp.arange(GRID_W), n_tok // GRID_W).astype(F32)
    n_freq = QK_DIM // 4
    inv = ROPE_BASE ** (-jnp.arange(n_freq, dtype=F32) / n_freq)
    ang = jnp.concatenate([r[:, None] * inv, col[:, None] * inv], axis=-1)
    cos, sin = jnp.cos(ang), jnp.sin(ang)
    reps = LANES // QK_DIM
    return jnp.tile(jnp.concatenate([cos, cos], -1), (1, reps)), jnp.tile(jnp.concatenate([-sin, sin], -1), (1, reps))


def _forward(x, c, ctx, c_ctx, w_ada, b_ada, norm1, norm2, w_in, w_out, q_norm, k_norm,
             lambda_q1, lambda_k1, lambda_q2, lambda_k2, subln, pool_w, pool_scale,
             conv_dw, conv_db, conv_ln_g, conv_ln_b, conv_pw, w_mlp1, w_mlp2, *, heads, tk):
    depth = w_in.shape[0]
    S, D = x.shape[1], x.shape[2]
    qw = heads * HEAD_W
    kv_off, pc_off = qw, 3 * qw
    pc_w = w_in.shape[2] - pc_off
    hidden = w_mlp1.shape[2]
    qblk = qw // HEAD_W

    xl = x[0]
    xc = ctx[0].astype(x.dtype)
    cos, sin = _rope_tables(S)
    gidx = jnp.arange(MXU_W) // QK_DIM
    gmat = (gidx[:, None] == gidx[None, :]).astype(BF16)

    cond8 = jnp.zeros((8, D), F32).at[0].set(c[0]).at[1].set(c_ctx)
    mods = _ada(cond8, w_ada, b_ada)

    for l in range(depth):
        last = l == depth - 1
        lam_init = 0.8 - 0.6 * math.exp(-0.3 * l)
        sh1, sc1, g1, sh2, sc2, g2 = jnp.split(mods[l, 0], N_MOD)
        csh1, csc1, cg1, csh2, csc2, cg2 = jnp.split(mods[l, 1], N_MOD)
        w_in_l, w_out_l = w_in[l].astype(BF16), w_out[l].astype(BF16)
        w1_l, w2_l = w_mlp1[l].astype(BF16), w_mlp2[l].astype(BF16)
        reps = qw // QK_DIM
        gain = jnp.concatenate([jnp.tile(q_norm[l], reps) * QK_DIM ** -0.5, jnp.tile(k_norm[l], reps)])[None, :]
        lamp = jnp.stack([lambda_q1[l], lambda_k1[l], lambda_q2[l], lambda_k2[l]]).astype(F32)
        mixer_w = (pool_w[l], pool_scale[l], conv_dw[l], conv_db[l], conv_ln_g[l], conv_ln_b[l], conv_pw[l])

        h = _prenorm(xl, norm1[l], sc1, sh1)
        hc = _prenorm(xc, norm1[l], csc1, csh1)
        qk = _mm_qk(h, w_in_l, gain, gmat, cos, sin)
        vt = _mm_vt(h, w_in_l, 2 * qw, qw, tk)
        upc = _mm_cast(h, w_in_l, pc_off, pc_w)
        qkc = _mm_qk(hc, w_in_l, gain, gmat)
        vct = _mm_vt(hc, w_in_l, 2 * qw, qw, tk)

        attn = _attention(lamp, subln[l], qk, 0, qkc, qblk, vct, qk, qblk, vt, lam_init=lam_init)
        pc = _mixer(upc, *mixer_w)
        x_new = _mm_resid([attn, pc], w_out_l, xl, g1, bk=qw)
        h2 = _prenorm(x_new, norm2[l], sc2, sh2)
        hid = _mm_cast(h2, w1_l, 0, hidden, relu2=True)
        x_new = _mm_resid([hid], w2_l, x_new, g2, bk=_blk(hidden, 2048), bn=1024)

        if not last:
            attn_c = _attention(lamp, subln[l], qkc, 0, qkc, qblk, vct, lam_init=lam_init)
            upcc = _mm_cast(hc, w_in_l, pc_off, pc_w)
            pcc = _mixer(upcc, *mixer_w)
            xc_new = _mm_resid([attn_c, pcc], w_out_l, xc, cg1, bk=qw)
            hc2 = _prenorm(xc_new, norm2[l], csc2, csh2)
            hidc = _mm_cast(hc2, w1_l, 0, hidden, relu2=True)
            xc = _mm_resid([hidc], w2_l, xc_new, cg2, bk=_blk(hidden, 2048), bn=1024)
        xl = x_new
    return xl[None]


def kernel(x, c, ctx, c_ctx, w_ada, b_ada, norm1, norm2, w_in, w_out, q_norm, k_norm, lambda_q1, lambda_k1,
           lambda_q2, lambda_k2, subln, pool_w, pool_scale, conv_dw, conv_db, conv_ln_g, conv_ln_b, conv_pw,
           w_mlp1, w_mlp2):
    return _forward(x, c, ctx, c_ctx, w_ada, b_ada, norm1, norm2, w_in, w_out, q_norm, k_norm,
                    lambda_q1, lambda_k1, lambda_q2, lambda_k2, subln, pool_w, pool_scale,
                    conv_dw, conv_db, conv_ln_g, conv_ln_b, conv_pw, w_mlp1, w_mlp2,
                    heads=ATTN_HEADS, tk=256)
```

```python
import functools
import math

import jax
import jax.numpy as jnp
from jax import lax
from jax.experimental import pallas as pl
from jax.experimental.pallas import tpu as pltpu

F32 = jnp.float32
BF16 = jnp.bfloat16

ATTN_HEADS = 16
QK_DIM = 64
HEAD_W = 2 * QK_DIM
V_ROWS = HEAD_W + 16
GRID_W = 64
N_MOD = 6
POOL_WINDOWS = (2, 4, 8, 16)
CONV_K = 31
HALO = 16
ROPE_BASE = 10000.0
EPS = 1e-6
QK_SCALE = QK_DIM ** -0.5 * math.log2(math.e)
BOUND_SLACK = 1.05
MAX_SCORE_BOUND = 60.0

LANES = 128
MXU_W = 256
VMEM_BYTES = 64 << 20


def _params(sem, vmem_mb):
    assert (vmem_mb << 20) <= VMEM_BYTES
    return pltpu.CompilerParams(dimension_semantics=sem, vmem_limit_bytes=vmem_mb << 20)


def _blk(n, pref):
    if n <= pref:
        return n
    b = pref
    while n % b:
        b //= 2
    return b


def _ada_kernel(cond_ref, w_ref, b_ref, o_ref):
    s = jax.nn.silu(cond_ref[...]).astype(BF16)
    o_ref[...] = jnp.dot(s, w_ref[...].astype(BF16), preferred_element_type=F32) + b_ref[...]


def _ada(cond8, w_ada, b_ada):
    depth, d, n = w_ada.shape
    bn = _blk(n, 512)
    return pl.pallas_call(
        _ada_kernel,
        grid=(depth, n // bn),
        in_specs=[
            pl.BlockSpec((8, d), lambda l, j: (0, 0)),
            pl.BlockSpec((None, d, bn), lambda l, j: (l, 0, j)),
            pl.BlockSpec((None, 1, bn), lambda l, j: (l, 0, j)),
        ],
        out_specs=pl.BlockSpec((None, 8, bn), lambda l, j: (l, 0, j)),
        out_shape=jax.ShapeDtypeStruct((depth, 8, n), F32),
        compiler_params=_params(("arbitrary", "arbitrary"), 40),
        name="ada_mod",
    )(cond8, w_ada, b_ada.reshape(depth, 1, n))


def _prenorm_kernel(x_ref, g_ref, sc_ref, sh_ref, o_ref):
    x = x_ref[...]
    y = x * lax.rsqrt(jnp.mean(x * x, axis=-1, keepdims=True) + EPS) * g_ref[...]
    o_ref[...] = (y * (1.0 + sc_ref[...]) + sh_ref[...]).astype(o_ref.dtype)


def _prenorm(x, g, scale, shift):
    L, d = x.shape
    bm = _blk(L, 512)
    row = pl.BlockSpec((1, d), lambda i: (0, 0))
    return pl.pallas_call(
        _prenorm_kernel,
        grid=(L // bm,),
        in_specs=[pl.BlockSpec((bm, d), lambda i: (i, 0)), row, row, row],
        out_specs=pl.BlockSpec((bm, d), lambda i: (i, 0)),
        out_shape=jax.ShapeDtypeStruct((L, d), BF16),
        compiler_params=_params(("arbitrary",), 40),
        name="prenorm",
    )(x, g.reshape(1, d), scale.reshape(1, d), shift.reshape(1, d))


def _mm_cast_kernel(a_ref, w_ref, o_ref, *, relu2):
    acc = jnp.dot(a_ref[...], w_ref[...], preferred_element_type=F32)
    if relu2:
        acc = jnp.square(jnp.maximum(acc, 0.0))
    o_ref[...] = acc.astype(o_ref.dtype)


def _mm_cast(a, w, col_off, n, *, relu2=False, bm=1024, bn=1024):
    M, K = a.shape
    bm, bn = _blk(M, bm), _blk(n, bn)
    assert col_off % bn == 0
    joff = col_off // bn
    return pl.pallas_call(
        functools.partial(_mm_cast_kernel, relu2=relu2),
        grid=(M // bm, n // bn),
        in_specs=[
            pl.BlockSpec((bm, K), lambda i, j: (i, 0)),
            pl.BlockSpec((K, bn), lambda i, j: (0, j + joff)),
        ],
        out_specs=pl.BlockSpec((bm, bn), lambda i, j: (i, j)),
        out_shape=jax.ShapeDtypeStruct((M, n), BF16),
        compiler_params=_params(("arbitrary", "arbitrary"), 48),
        name="proj_relu2" if relu2 else "proj_cast",
    )(a, w)


def _mm_vt_kernel(a_ref, w_ref, o_ref, *, tk):
    acc = jnp.dot(a_ref[...], w_ref[...], preferred_element_type=F32)
    ones = jnp.ones((V_ROWS - HEAD_W, tk), o_ref.dtype)
    for c in range(a_ref.shape[0] // tk):
        for h in range(acc.shape[1] // HEAD_W):
            vt = acc[c * tk:(c + 1) * tk, h * HEAD_W:(h + 1) * HEAD_W].T
            o_ref[c, h * V_ROWS:h * V_ROWS + HEAD_W, :] = vt.astype(o_ref.dtype)
            o_ref[c, h * V_ROWS + HEAD_W:(h + 1) * V_ROWS, :] = ones


def _mm_vt(a, w, col_off, n, tk, *, bm=1024, bn=512):
    M, K = a.shape
    bm, bn = _blk(M, bm), _blk(n, bn)
    assert bm % tk == 0 and col_off % bn == 0 and bn % HEAD_W == 0
    joff = col_off // bn
    bn_out, n_out = bn // HEAD_W * V_ROWS, n // HEAD_W * V_ROWS
    return pl.pallas_call(
        functools.partial(_mm_vt_kernel, tk=tk),
        grid=(M // bm, n // bn),
        in_specs=[
            pl.BlockSpec((bm, K), lambda i, j: (i, 0)),
            pl.BlockSpec((K, bn), lambda i, j: (0, j + joff)),
        ],
        out_specs=pl.BlockSpec((bm // tk, bn_out, tk), lambda i, j: (i, j, 0)),
        out_shape=jax.ShapeDtypeStruct((M // tk, n_out, tk), BF16),
        compiler_params=_params(("arbitrary", "arbitrary"), 48),
        name="proj_vt",
    )(a, w)


def _mm_qk_kernel(a_ref, w_ref, gain_ref, gmat_ref, *rest, rope):
    if rope:
        cos_ref, sin_ref, o_ref = rest
    else:
        (o_ref,) = rest
    acc = jnp.dot(a_ref[...], w_ref[...], preferred_element_type=F32)
    bm, bn = acc.shape
    if rope:
        cos, sin = cos_ref[...], sin_ref[...]
        low_half = (lax.broadcasted_iota(jnp.int32, (bm, LANES), 1) & (QK_DIM // 2)) == 0
    for c in range(bn // MXU_W):
        t = acc[:, c * MXU_W:(c + 1) * MXU_W]
        ss = jnp.dot((t * t).astype(BF16), gmat_ref[...], preferred_element_type=F32)
        y = t * lax.rsqrt(ss * (1.0 / QK_DIM) + EPS) * gain_ref[:, c * MXU_W:(c + 1) * MXU_W]
        for h in range(MXU_W // LANES):
            yy = y[:, h * LANES:(h + 1) * LANES]
            if rope:
                partner = jnp.where(low_half, pltpu.roll(yy, LANES - QK_DIM // 2, 1), pltpu.roll(yy, QK_DIM // 2, 1))
                yy = yy * cos + partner * sin
            lo = c * MXU_W + h * LANES
            o_ref[:, lo:lo + LANES] = yy.astype(o_ref.dtype)


def _mm_qk(a, w, gain_row, gmat, cos=None, sin=None, *, bm=1024, bn=1024):
    M, K = a.shape
    n = gain_row.shape[1]
    bm, bn = _blk(M, bm), _blk(n, bn)
    rope = cos is not None
    in_specs = [
        pl.BlockSpec((bm, K), lambda i, j: (i, 0)),
        pl.BlockSpec((K, bn), lambda i, j: (0, j)),
        pl.BlockSpec((1, bn), lambda i, j: (0, j)),
        pl.BlockSpec((MXU_W, MXU_W), lambda i, j: (0, 0)),
    ]
    args = [a, w, gain_row, gmat]
    if rope:
        in_specs += [pl.BlockSpec((bm, LANES), lambda i, j: (i, 0))] * 2
        args += [cos, sin]
    return pl.pallas_call(
        functools.partial(_mm_qk_kernel, rope=rope),
        grid=(M // bm, n // bn),
        in_specs=in_specs,
        out_specs=pl.BlockSpec((bm, bn), lambda i, j: (i, j)),
        out_shape=jax.ShapeDtypeStruct((M, n), BF16),
        compiler_params=_params(("arbitrary", "arbitrary"), 52),
        name="proj_qk",
    )(*args)


def _mm_resid_kernel(*refs, n_lhs):
    a_refs, w_refs = refs[:n_lhs], refs[n_lhs:2 * n_lhs]
    x_ref, g_ref, o_ref, acc_ref = refs[2 * n_lhs:]
    k = pl.program_id(2)
    part = jnp.dot(a_refs[0][...], w_refs[0][...], preferred_element_type=F32)
    for a_ref, w_ref in zip(a_refs[1:], w_refs[1:]):
        part += jnp.dot(a_ref[...], w_ref[...], preferred_element_type=F32)

    @pl.when(k == 0)
    def _():
        acc_ref[...] = part

    @pl.when(k > 0)
    def _():
        acc_ref[...] += part

    @pl.when(k == pl.num_programs(2) - 1)
    def _():
        o_ref[...] = x_ref[...] + g_ref[...] * acc_ref[...]


def _mm_resid(lhs, w, x, gate, *, bk, bm=1024, bn=512):
    M, n = x.shape
    kw = lhs[0].shape[1]
    assert all(a.shape == (M, kw) for a in lhs) and w.shape == (kw * len(lhs), n) and kw % bk == 0
    bm, bn = _blk(M, bm), _blk(n, bn)
    ksteps = kw // bk
    in_specs = [pl.BlockSpec((bm, bk), lambda i, j, k: (i, k)) for _ in lhs]
    in_specs += [pl.BlockSpec((bk, bn), functools.partial(lambda i, j, k, s: (s * ksteps + k, j), s=s))
                 for s in range(len(lhs))]
    in_specs += [pl.BlockSpec((bm, bn), lambda i, j, k: (i, j)), pl.BlockSpec((1, bn), lambda i, j, k: (0, j))]
    return pl.pallas_call(
        functools.partial(_mm_resid_kernel, n_lhs=len(lhs)),
        grid=(M // bm, n // bn, ksteps),
        in_specs=in_specs,
        out_specs=pl.BlockSpec((bm, bn), lambda i, j, k: (i, j)),
        out_shape=jax.ShapeDtypeStruct((M, n), F32),
        scratch_shapes=[pltpu.VMEM((bm, bn), F32)],
        compiler_params=_params(("arbitrary", "arbitrary", "arbitrary"), 48),
        name="proj_resid",
    )(*lhs, *([w] * len(lhs)), x, gate.reshape(1, n))


_NT = (((1,), (1,)), ((), ()))


def _attn_kernel(*refs, tq, n_lat, lam_init):
    if n_lat:
        lamp_ref, subln_ref, q_ref, kc_ref, vc_ref, k_ref, v_ref, o_ref, qz_ref, m_ref, acc_ref = refs[:11]
        tk = v_ref.shape[2]
    else:
        lamp_ref, subln_ref, q_ref, kc_ref, vc_ref, o_ref, qz_ref, m_ref, acc_ref = refs[:9]
        tk = 0
    s_refs, p_refs, cm_refs = refs[-6:-4], refs[-4:-2], refs[-2:]
    tc = kc_ref.shape[0]
    n_chunks = 1 + n_lat

    _attn_fill_qz(q_ref, qz_ref, tq)
    m_ref[...] = jnp.full(m_ref.shape, -jnp.inf, F32)
    acc_ref[...] = jnp.zeros(acc_ref.shape, F32)

    def scores(k_blk, slot):
        s = lax.dot_general(k_blk, qz_ref[...], _NT, preferred_element_type=F32)
        s_refs[slot][0:k_blk.shape[0], :] = s
        cm_refs[slot][...] = jnp.max(s, axis=0, keepdims=True)

    def values(c, slot):
        if isinstance(c, int) and c == 0:
            return jnp.dot(vc_ref[0], p_refs[slot][0:tc, :], preferred_element_type=F32)
        return jnp.dot(v_ref[c - 1], p_refs[slot][0:tk, :], preferred_element_type=F32)

    def iteration(c, slot, *, first, last):
        if not last:
            scores(k_ref[pl.ds(pl.multiple_of(c * tk, tk), tk), :], 1 - slot)
        pv = None if first else values(c - 1, 1 - slot)
        n = tc if first else tk
        m_old = m_ref[...]
        m_new = jnp.maximum(m_old, cm_refs[slot][...])
        p_refs[slot][0:n, :] = jnp.exp2(s_refs[slot][0:n, :] - m_new).astype(BF16)
        m_ref[...] = m_new
        if not first:
            acc_ref[...] = jnp.exp2(m_old - m_new) * (acc_ref[...] + pv)

    scores(kc_ref[...], 0)
    for c in range(min(2, n_chunks)):
        iteration(c, c % 2, first=c == 0, last=c == n_chunks - 1)
    steady = list(range(2, n_chunks - 1))
    if len(steady) >= 2:
        def pair(j, carry):
            c = 2 + 2 * j
            iteration(c, 0, first=False, last=False)
            iteration(c + 1, 1, first=False, last=False)
            return carry
        lax.fori_loop(0, len(steady) // 2, pair, 0)
    if len(steady) % 2:
        iteration(steady[-1], steady[-1] % 2, first=False, last=False)
    if n_chunks > 2:
        iteration(n_chunks - 1, (n_chunks - 1) % 2, first=False, last=True)
    acc = acc_ref[...] + values(n_chunks - 1, (n_chunks - 1) % 2)
    _attn_finish(acc, lamp_ref, subln_ref, o_ref, tq, lam_init)


def _attn_finish(acc, lamp_ref, subln_ref, o_ref, tq, lam_init):
    lp = lamp_ref[...]
    lam = (jnp.exp(jnp.sum(lp[0:1] * lp[1:2], axis=1, keepdims=True))
           - jnp.exp(jnp.sum(lp[2:3] * lp[3:4], axis=1, keepdims=True)) + lam_init)
    o = acc[:HEAD_W] / acc[HEAD_W:HEAD_W + 1]
    d = o[:, :tq] - lam * o[:, tq:]
    d = d * lax.rsqrt(jnp.mean(d * d, axis=0, keepdims=True) + EPS)
    o_ref[...] = (d.T * subln_ref[...] * (1.0 - lam_init)).astype(o_ref.dtype)


def _attn_bounded_kernel(mb_ref, lamp_ref, subln_ref, q_ref, kc_ref, vc_ref, k_ref, v_ref, o_ref,
                         qz_ref, acc_ref, p0_ref, p1_ref, *, tq, lam_init):
    n_lat, _, tk = v_ref.shape
    tc = kc_ref.shape[0]
    n_chunks = 1 + n_lat
    p_refs = (p0_ref, p1_ref)
    _attn_fill_qz(q_ref, qz_ref, tq)
    mb = mb_ref[...]

    def probs(k_blk, slot):
        s = lax.dot_general(k_blk, qz_ref[...], _NT, preferred_element_type=F32)
        p_refs[slot][0:k_blk.shape[0], :] = jnp.exp2(s - mb).astype(BF16)

    def iteration(c, slot, *, first, last):
        if not last:
            probs(k_ref[pl.ds(pl.multiple_of(c * tk, tk), tk), :], 1 - slot)
        if first:
            acc_ref[...] = jnp.dot(vc_ref[0], p_refs[slot][0:tc, :], preferred_element_type=F32)
        else:
            acc_ref[...] += jnp.dot(v_ref[c - 1], p_refs[slot][0:tk, :], preferred_element_type=F32)

    probs(kc_ref[...], 0)
    iteration(0, 0, first=True, last=False)
    steady = list(range(1, n_chunks - 1))
    if len(steady) >= 2:
        def pair(j, carry):
            c = 1 + 2 * j
            iteration(c, 1, first=False, last=False)
            iteration(c + 1, 0, first=False, last=False)
            return carry
        lax.fori_loop(0, len(steady) // 2, pair, 0)
    if len(steady) % 2:
        iteration(steady[-1], steady[-1] % 2, first=False, last=False)
    iteration(n_chunks - 1, (n_chunks - 1) % 2, first=False, last=True)
    _attn_finish(acc_ref[...], lamp_ref, subln_ref, o_ref, tq, lam_init)


def _attn_fill_qz(q_ref, qz_ref, tq):
    q = q_ref[...]
    lane = lax.broadcasted_iota(jnp.int32, q.shape, 1)
    qz_ref[0:tq, :] = jnp.where(lane < QK_DIM, q, jnp.zeros_like(q))
    qz_ref[tq:2 * tq, :] = jnp.where(lane >= QK_DIM, q, jnp.zeros_like(q))


def _attention(lamp, subln, q_arr, q_col0, kc_arr, kc_col0, vct, k_arr=None, k_col0=0, vt=None, score_bound=None,
               *, lam_init, tq=512):
    Lq = q_arr.shape[0]
    C = kc_arr.shape[0]
    heads = vct.shape[1] // V_ROWS
    assert vct.shape == (1, heads * V_ROWS, C)
    hw = heads * HEAD_W
    tq = _blk(Lq, tq)
    n_lat, tk = (0, 0) if k_arr is None else (vt.shape[0], vt.shape[2])
    in_specs = [
        pl.BlockSpec((4, QK_DIM), lambda h, i: (0, 0)),
        pl.BlockSpec((1, HEAD_W), lambda h, i: (0, 0)),
        pl.BlockSpec((tq, HEAD_W), lambda h, i: (i, q_col0 + h)),
        pl.BlockSpec((C, HEAD_W), lambda h, i: (0, kc_col0 + h)),
        pl.BlockSpec((1, V_ROWS, C), lambda h, i: (0, h, 0)),
    ]
    args = [lamp, subln.reshape(1, HEAD_W), q_arr, kc_arr, vct]
    if n_lat:
        assert k_arr.shape[0] == n_lat * tk
        in_specs += [
            pl.BlockSpec((k_arr.shape[0], HEAD_W), lambda h, i: (0, k_col0 + h)),
            pl.BlockSpec((n_lat, V_ROWS, tk), lambda h, i: (0, h, 0)),
        ]
        args += [k_arr, vt]
    tmax = max(C, tk)
    per_slot = lambda shape, dtype: [pltpu.VMEM(shape, dtype)] * 2
    common = dict(
        grid=(heads, Lq // tq),
        out_specs=pl.BlockSpec((tq, HEAD_W), lambda h, i: (i, h)),
        out_shape=jax.ShapeDtypeStruct((Lq, hw), BF16),
        compiler_params=_params(("arbitrary", "arbitrary"), 48),
    )
    general = pl.pallas_call(
        functools.partial(_attn_kernel, tq=tq, n_lat=n_lat, lam_init=lam_init),
        in_specs=in_specs,
        scratch_shapes=[
            pltpu.VMEM((2 * tq, HEAD_W), BF16),
            pltpu.VMEM((1, 2 * tq), F32),
            pltpu.VMEM((V_ROWS, 2 * tq), F32),
            *per_slot((tmax, 2 * tq), F32),
            *per_slot((tmax, 2 * tq), BF16),
            *per_slot((1, 2 * tq), F32),
        ],
        name="diff_attn",
        **common,
    )
    if score_bound is None:
        return general(*args)
    bounded = pl.pallas_call(
        functools.partial(_attn_bounded_kernel, tq=tq, lam_init=lam_init),
        in_specs=[pl.BlockSpec((1, 1), lambda h, i: (0, 0))] + in_specs,
        scratch_shapes=[
            pltpu.VMEM((2 * tq, HEAD_W), BF16),
            pltpu.VMEM((V_ROWS, 2 * tq), F32),
            *per_slot((tmax, 2 * tq), BF16),
        ],
        name="diff_attn_bounded",
        **common,
    )
    return lax.cond(score_bound <= MAX_SCORE_BOUND,
                    lambda *a: bounded(score_bound.reshape(1, 1), *a), general, *args)


def _mixer_kernel(u_ref, up_ref, un_ref, pw_ref, ps_ref, dw_ref, db_ref, lg_ref, lb_ref, cw_ref, o_ref,
                  pool_ref, glu_ref, *, seq_len):
    bm = u_ref.shape[0]
    pw_cols = pool_ref.shape[1]
    grp = pw_cols // len(POOL_WINDOWS)
    i = pl.program_id(0)
    has_prev = (i > 0).astype(F32)
    has_next = (i < pl.num_programs(0) - 1).astype(F32)

    def fill(row0, src, valid):
        u = src.astype(F32)
        pool_ref[row0:row0 + src.shape[0], :] = u[:, :pw_cols] * valid
        a, gate = u[:, pw_cols:2 * pw_cols], u[:, 2 * pw_cols:]
        glu_ref[row0:row0 + src.shape[0], :] = a * jax.nn.sigmoid(gate) * valid

    fill(0, up_ref[...], has_prev)
    fill(HALO, u_ref[...], 1.0)
    fill(HALO + bm, un_ref[...], has_next)

    t = i * bm + lax.broadcasted_iota(jnp.int32, (bm, 1), 0)
    for g, w in enumerate(POOL_WINDOWS):
        cols = slice(g * grp, (g + 1) * grp)
        tot = pool_ref[HALO - w // 2:HALO - w // 2 + bm, cols]
        for dlt in range(-w // 2 + 1, w - w // 2):
            tot = tot + pool_ref[HALO + dlt:HALO + dlt + bm, cols]
        cnt = (jnp.clip(t + (w - w // 2), 0, seq_len) - jnp.clip(t - w // 2, 0, seq_len)).astype(F32)
        pooled = tot / cnt - pool_ref[HALO:HALO + bm, cols]
        y = jnp.dot(pooled.astype(BF16), pw_ref[g], preferred_element_type=F32) * ps_ref[:, cols]
        o_ref[:, cols] = y.astype(o_ref.dtype)

    half = CONV_K // 2
    y = glu_ref[HALO - half:HALO - half + bm, :] * dw_ref[0:1, :]
    for k in range(1, CONV_K):
        y = y + glu_ref[HALO - half + k:HALO - half + k + bm, :] * dw_ref[k:k + 1, :]
    y = y + db_ref[...]
    mu = jnp.mean(y, axis=-1, keepdims=True)
    yc = y - mu
    var = jnp.mean(yc * yc, axis=-1, keepdims=True)
    z = yc * lax.rsqrt(var + EPS) * lg_ref[...] + lb_ref[...]
    z = jax.nn.silu(z)
    o_ref[:, pw_cols:] = jnp.dot(z.astype(BF16), cw_ref[...], preferred_element_type=F32).astype(o_ref.dtype)


def _mixer(upc, pool_w, pool_scale, conv_dw, conv_db, ln_g, ln_b, conv_pw, *, bm=256):
    L, w3 = upc.shape
    cw = w3 // 3
    bm = _blk(L, bm)
    nb, hb = L // bm, bm // HALO
    last_halo = L // HALO - 1
    dw_pad = jnp.zeros((32, cw), F32).at[:CONV_K].set(conv_dw)
    row = pl.BlockSpec((1, cw), lambda i: (0, 0))
    return pl.pallas_call(
        functools.partial(_mixer_kernel, seq_len=L),
        grid=(nb,),
        in_specs=[
            pl.BlockSpec((bm, w3), lambda i: (i, 0)),
            pl.BlockSpec((HALO, w3), lambda i: (jnp.maximum(i * hb - 1, 0), 0)),
            pl.BlockSpec((HALO, w3), lambda i: (jnp.minimum((i + 1) * hb, last_halo), 0)),
            pl.BlockSpec(pool_w.shape, lambda i: (0, 0, 0)),
            row,
            pl.BlockSpec((32, cw), lambda i: (0, 0)),
            row, row, row,
            pl.BlockSpec((cw, cw), lambda i: (0, 0)),
        ],
        out_specs=pl.BlockSpec((bm, 2 * cw), lambda i: (i, 0)),
        out_shape=jax.ShapeDtypeStruct((L, 2 * cw), BF16),
        scratch_shapes=[pltpu.VMEM((bm + 2 * HALO, cw), F32), pltpu.VMEM((bm + 2 * HALO, cw), F32)],
        compiler_params=_params(("arbitrary",), 40),
        name="pool_conv_mixer",
    )(upc, upc, upc, pool_w.astype(BF16), pool_scale.reshape(1, cw), dw_pad, conv_db.reshape(1, cw),
      ln_g.reshape(1, cw), ln_b.reshape(1, cw), conv_pw.astype(BF16))


def _rope_tables(n_tok):
    r = jnp.repeat(jnp.arange(n_tok // GRID_W), GRID_W).astype(F32)
    col = jnp.tile(jnp.arange(GRID_W), n_tok // GRID_W).astype(F32)
    n_freq = QK_DIM // 4
    inv = ROPE_BASE ** (-jnp.arange(n_freq, dtype=F32) / n_freq)
    ang = jnp.concatenate([r[:, None] * inv, col[:, None] * inv], axis=-1)
    cos, sin = jnp.cos(ang), jnp.sin(ang)
    reps = LANES // QK_DIM
    return jnp.tile(jnp.concatenate([cos, cos], -1), (1, reps)), jnp.tile(jnp.concatenate([-sin, sin], -1), (1, reps))


def _forward(x, c, ctx, c_ctx, w_ada, b_ada, norm1, norm2, w_in, w_out, q_norm, k_norm,
             lambda_q1, lambda_k1, lambda_q2, lambda_k2, subln, pool_w, pool_scale,
             conv_dw, conv_db, conv_ln_g, conv_ln_b, conv_pw, w_mlp1, w_mlp2, *, heads, tk):
    depth = w_in.shape[0]
    S, D = x.shape[1], x.shape[2]
    qw = heads * HEAD_W
    kv_off, pc_off = qw, 3 * qw
    pc_w = w_in.shape[2] - pc_off
    hidden = w_mlp1.shape[2]
    qblk = qw // HEAD_W

    xl = x[0]
    xc = ctx[0].astype(x.dtype)
    cos, sin = _rope_tables(S)
    gidx = jnp.arange(MXU_W) // QK_DIM
    gmat = (gidx[:, None] == gidx[None, :]).astype(BF16)

    cond8 = jnp.zeros((8, D), F32).at[0].set(c[0]).at[1].set(c_ctx)
    mods = _ada(cond8, w_ada, b_ada)

    for l in range(depth):
        last = l == depth - 1
        lam_init = 0.8 - 0.6 * math.exp(-0.3 * l)
        sh1, sc1, g1, sh2, sc2, g2 = jnp.split(mods[l, 0], N_MOD)
        csh1, csc1, cg1, csh2, csc2, cg2 = jnp.split(mods[l, 1], N_MOD)
        w_in_l, w_out_l = w_in[l].astype(BF16), w_out[l].astype(BF16)
        w1_l, w2_l = w_mlp1[l].astype(BF16), w_mlp2[l].astype(BF16)
        reps = qw // QK_DIM
        gain = jnp.concatenate([jnp.tile(q_norm[l], reps) * QK_SCALE,
                                jnp.tile(k_norm[l], reps)])[None, :]
        lamp = jnp.stack([lambda_q1[l], lambda_k1[l], lambda_q2[l], lambda_k2[l]]).astype(F32)
        mixer_w = (pool_w[l], pool_scale[l], conv_dw[l], conv_db[l], conv_ln_g[l], conv_ln_b[l], conv_pw[l])

        h = _prenorm(xl, norm1[l], sc1, sh1)
        hc = _prenorm(xc, norm1[l], csc1, csh1)
        qk = _mm_qk(h, w_in_l, gain, gmat, cos, sin)
        vt = _mm_vt(h, w_in_l, 2 * qw, qw, tk)
        upc = _mm_cast(h, w_in_l, pc_off, pc_w)
        qkc = _mm_qk(hc, w_in_l, gain, gmat)
        vct = _mm_vt(hc, w_in_l, 2 * qw, qw, xc.shape[0])

        score_bound = (QK_DIM * QK_SCALE * BOUND_SLACK) * jnp.max(jnp.abs(q_norm[l])) * jnp.max(jnp.abs(k_norm[l]))
        attn = _attention(lamp, subln[l], qk, 0, qkc, qblk, vct, qk, qblk, vt, score_bound.astype(F32),
                          lam_init=lam_init)
        pc = _mixer(upc, *mixer_w)
        x_new = _mm_resid([attn, pc], w_out_l, xl, g1, bk=qw)
        h2 = _prenorm(x_new, norm2[l], sc2, sh2)
        hid = _mm_cast(h2, w1_l, 0, hidden, relu2=True)
        x_new = _mm_resid([hid], w2_l, x_new, g2, bk=_blk(hidden, 2048), bn=1024)

        if not last:
            attn_c = _attention(lamp, subln[l], qkc, 0, qkc, qblk, vct, lam_init=lam_init)
            upcc = _mm_cast(hc, w_in_l, pc_off, pc_w)
            pcc = _mixer(upcc, *mixer_w)
            xc_new = _mm_resid([attn_c, pcc], w_out_l, xc, cg1, bk=qw)
            hc2 = _prenorm(xc_new, norm2[l], csc2, csh2)
            hidc = _mm_cast(hc2, w1_l, 0, hidden, relu2=True)
            xc = _mm_resid([hidc], w2_l, xc_new, cg2, bk=_blk(hidden, 2048), bn=1024)
        xl = x_new
    return xl[None]


def kernel(x, c, ctx, c_ctx, w_ada, b_ada, norm1, norm2, w_in, w_out, q_norm, k_norm, lambda_q1, lambda_k1,
           lambda_q2, lambda_k2, subln, pool_w, pool_scale, conv_dw, conv_db, conv_ln_g, conv_ln_b, conv_pw,
           w_mlp1, w_mlp2):
    return _forward(x, c, ctx, c_ctx, w_ada, b_ada, norm1, norm2, w_in, w_out, q_norm, k_norm,
                    lambda_q1, lambda_k1, lambda_q2, lambda_k2, subln, pool_w, pool_scale,
                    conv_dw, conv_db, conv_ln_g, conv_ln_b, conv_pw, w_mlp1, w_mlp2,
                    heads=ATTN_HEADS, tk=512)
```

```python
import functools
import math

import jax
import jax.numpy as jnp
from jax import lax
from jax.experimental import pallas as pl
from jax.experimental.pallas import tpu as pltpu

F32 = jnp.float32
BF16 = jnp.bfloat16

ATTN_HEADS = 16
QK_DIM = 64
HEAD_W = 2 * QK_DIM
V_ROWS = HEAD_W + 16
GRID_W = 64
N_MOD = 6
POOL_WINDOWS = (2, 4, 8, 16)
CONV_K = 31
HALO = 16
ROPE_BASE = 10000.0
EPS = 1e-6
QK_SCALE = QK_DIM ** -0.5 * math.log2(math.e)
BOUND_SLACK = 1.05
MAX_SCORE_BOUND = 60.0

LANES = 128
MXU_W = 256
VMEM_BYTES = 64 << 20


def _params(sem, vmem_mb):
    assert (vmem_mb << 20) <= VMEM_BYTES
    return pltpu.CompilerParams(dimension_semantics=sem, vmem_limit_bytes=vmem_mb << 20)


def _blk(n, pref):
    if n <= pref:
        return n
    b = pref
    while n % b:
        b //= 2
    return b


def _ada_kernel(cond_ref, w_ref, b_ref, o_ref):
    s = jax.nn.silu(cond_ref[...]).astype(BF16)
    o_ref[...] = jnp.dot(s, w_ref[...].astype(BF16), preferred_element_type=F32) + b_ref[...]


def _ada(cond8, w_ada, b_ada):
    depth, d, n = w_ada.shape
    bn = _blk(n, 512)
    return pl.pallas_call(
        _ada_kernel,
        grid=(depth, n // bn),
        in_specs=[
            pl.BlockSpec((8, d), lambda l, j: (0, 0)),
            pl.BlockSpec((None, d, bn), lambda l, j: (l, 0, j)),
            pl.BlockSpec((None, 1, bn), lambda l, j: (l, 0, j)),
        ],
        out_specs=pl.BlockSpec((None, 8, bn), lambda l, j: (l, 0, j)),
        out_shape=jax.ShapeDtypeStruct((depth, 8, n), F32),
        compiler_params=_params(("arbitrary", "arbitrary"), 40),
        name="ada_mod",
    )(cond8, w_ada, b_ada.reshape(depth, 1, n))


def _prenorm_kernel(x_ref, g_ref, sc_ref, sh_ref, o_ref):
    x = x_ref[...]
    y = x * lax.rsqrt(jnp.mean(x * x, axis=-1, keepdims=True) + EPS) * g_ref[...]
    o_ref[...] = (y * (1.0 + sc_ref[...]) + sh_ref[...]).astype(o_ref.dtype)


def _prenorm(x, g, scale, shift):
    L, d = x.shape
    bm = _blk(L, 512)
    row = pl.BlockSpec((1, d), lambda i: (0, 0))
    return pl.pallas_call(
        _prenorm_kernel,
        grid=(L // bm,),
        in_specs=[pl.BlockSpec((bm, d), lambda i: (i, 0)), row, row, row],
        out_specs=pl.BlockSpec((bm, d), lambda i: (i, 0)),
        out_shape=jax.ShapeDtypeStruct((L, d), BF16),
        compiler_params=_params(("arbitrary",), 40),
        name="prenorm",
    )(x, g.reshape(1, d), scale.reshape(1, d), shift.reshape(1, d))


def _mm_cast_kernel(a_ref, w_ref, o_ref, *, relu2):
    acc = jnp.dot(a_ref[...], w_ref[...], preferred_element_type=F32)
    if relu2:
        acc = jnp.square(jnp.maximum(acc, 0.0))
    o_ref[...] = acc.astype(o_ref.dtype)


def _mm_cast(a, w, col_off, n, *, relu2=False, bm=1024, bn=1024):
    M, K = a.shape
    bm, bn = _blk(M, bm), _blk(n, bn)
    assert col_off % bn == 0
    joff = col_off // bn
    return pl.pallas_call(
        functools.partial(_mm_cast_kernel, relu2=relu2),
        grid=(M // bm, n // bn),
        in_specs=[
            pl.BlockSpec((bm, K), lambda i, j: (i, 0)),
            pl.BlockSpec((K, bn), lambda i, j: (0, j + joff)),
        ],
        out_specs=pl.BlockSpec((bm, bn), lambda i, j: (i, j)),
        out_shape=jax.ShapeDtypeStruct((M, n), BF16),
        compiler_params=_params(("arbitrary", "arbitrary"), 48),
        name="proj_relu2" if relu2 else "proj_cast",
    )(a, w)


def _mm_vt_kernel(a_ref, w_ref, o_ref, *, tk):
    acc = jnp.dot(a_ref[...], w_ref[...], preferred_element_type=F32)
    ones = jnp.ones((V_ROWS - HEAD_W, tk), o_ref.dtype)
    for c in range(a_ref.shape[0] // tk):
        for h in range(acc.shape[1] // HEAD_W):
            vt = acc[c * tk:(c + 1) * tk, h * HEAD_W:(h + 1) * HEAD_W].T
            o_ref[c, h * V_ROWS:h * V_ROWS + HEAD_W, :] = vt.astype(o_ref.dtype)
            o_ref[c, h * V_ROWS + HEAD_W:(h + 1) * V_ROWS, :] = ones


def _mm_vt(a, w, col_off, n, tk, *, bm=1024, bn=512):
    M, K = a.shape
    bm, bn = _blk(M, bm), _blk(n, bn)
    assert bm % tk == 0 and col_off % bn == 0 and bn % HEAD_W == 0
    joff = col_off // bn
    bn_out, n_out = bn // HEAD_W * V_ROWS, n // HEAD_W * V_ROWS
    return pl.pallas_call(
        functools.partial(_mm_vt_kernel, tk=tk),
        grid=(M // bm, n // bn),
        in_specs=[
            pl.BlockSpec((bm, K), lambda i, j: (i, 0)),
            pl.BlockSpec((K, bn), lambda i, j: (0, j + joff)),
        ],
        out_specs=pl.BlockSpec((bm // tk, bn_out, tk), lambda i, j: (i, j, 0)),
        out_shape=jax.ShapeDtypeStruct((M // tk, n_out, tk), BF16),
        compiler_params=_params(("arbitrary", "arbitrary"), 48),
        name="proj_vt",
    )(a, w)


def _mm_qk_kernel(a_ref, w_ref, gain_ref, gmat_ref, *rest, rope):
    if rope:
        cos_ref, sin_ref, o_ref = rest
    else:
        (o_ref,) = rest
    acc = jnp.dot(a_ref[...], w_ref[...], preferred_element_type=F32)
    bm, bn = acc.shape
    if rope:
        cos, sin = cos_ref[...], sin_ref[...]
        low_half = (lax.broadcasted_iota(jnp.int32, (bm, LANES), 1) & (QK_DIM // 2)) == 0
    for c in range(bn // MXU_W):
        t = acc[:, c * MXU_W:(c + 1) * MXU_W]
        ss = jnp.dot((t * t).astype(BF16), gmat_ref[...], preferred_element_type=F32)
        y = t * lax.rsqrt(ss * (1.0 / QK_DIM) + EPS) * gain_ref[:, c * MXU_W:(c + 1) * MXU_W]
        for h in range(MXU_W // LANES):
            yy = y[:, h * LANES:(h + 1) * LANES]
            if rope:
                partner = jnp.where(low_half, pltpu.roll(yy, LANES - QK_DIM // 2, 1), pltpu.roll(yy, QK_DIM // 2, 1))
                yy = yy * cos + partner * sin
            lo = c * MXU_W + h * LANES
            o_ref[:, lo:lo + LANES] = yy.astype(o_ref.dtype)


def _mm_qk(a, w, gain_row, gmat, cos=None, sin=None, *, bm=1024, bn=1024):
    M, K = a.shape
    n = gain_row.shape[1]
    bm, bn = _blk(M, bm), _blk(n, bn)
    rope = cos is not None
    in_specs = [
        pl.BlockSpec((bm, K), lambda i, j: (i, 0)),
        pl.BlockSpec((K, bn), lambda i, j: (0, j)),
        pl.BlockSpec((1, bn), lambda i, j: (0, j)),
        pl.BlockSpec((MXU_W, MXU_W), lambda i, j: (0, 0)),
    ]
    args = [a, w, gain_row, gmat]
    if rope:
        in_specs += [pl.BlockSpec((bm, LANES), lambda i, j: (i, 0))] * 2
        args += [cos, sin]
    return pl.pallas_call(
        functools.partial(_mm_qk_kernel, rope=rope),
        grid=(M // bm, n // bn),
        in_specs=in_specs,
        out_specs=pl.BlockSpec((bm, bn), lambda i, j: (i, j)),
        out_shape=jax.ShapeDtypeStruct((M, n), BF16),
        compiler_params=_params(("arbitrary", "arbitrary"), 52),
        name="proj_qk",
    )(*args)


def _mm_resid_kernel(*refs, n_lhs):
    a_refs, w_refs = refs[:n_lhs], refs[n_lhs:2 * n_lhs]
    x_ref, g_ref, o_ref, acc_ref = refs[2 * n_lhs:]
    k = pl.program_id(2)
    part = jnp.dot(a_refs[0][...], w_refs[0][...], preferred_element_type=F32)
    for a_ref, w_ref in zip(a_refs[1:], w_refs[1:]):
        part += jnp.dot(a_ref[...], w_ref[...], preferred_element_type=F32)

    @pl.when(k == 0)
    def _():
        acc_ref[...] = part

    @pl.when(k > 0)
    def _():
        acc_ref[...] += part

    @pl.when(k == pl.num_programs(2) - 1)
    def _():
        o_ref[...] = x_ref[...] + g_ref[...] * acc_ref[...]


def _mm_resid(lhs, w, x, gate, *, bk, bm=1024, bn=512):
    M, n = x.shape
    kw = lhs[0].shape[1]
    assert all(a.shape == (M, kw) for a in lhs) and w.shape == (kw * len(lhs), n) and kw % bk == 0
    bm, bn = _blk(M, bm), _blk(n, bn)
    ksteps = kw // bk
    in_specs = [pl.BlockSpec((bm, bk), lambda i, j, k: (i, k)) for _ in lhs]
    in_specs += [pl.BlockSpec((bk, bn), functools.partial(lambda i, j, k, s: (s * ksteps + k, j), s=s))
                 for s in range(len(lhs))]
    in_specs += [pl.BlockSpec((bm, bn), lambda i, j, k: (i, j)), pl.BlockSpec((1, bn), lambda i, j, k: (0, j))]
    return pl.pallas_call(
        functools.partial(_mm_resid_kernel, n_lhs=len(lhs)),
        grid=(M // bm, n // bn, ksteps),
        in_specs=in_specs,
        out_specs=pl.BlockSpec((bm, bn), lambda i, j, k: (i, j)),
        out_shape=jax.ShapeDtypeStruct((M, n), F32),
        scratch_shapes=[pltpu.VMEM((bm, bn), F32)],
        compiler_params=_params(("arbitrary", "arbitrary", "arbitrary"), 56),
        name="proj_resid",
    )(*lhs, *([w] * len(lhs)), x, gate.reshape(1, n))


_NT = (((1,), (1,)), ((), ()))


def _attn_kernel(*refs, tq, n_lat, lam_init):
    if n_lat:
        lamp_ref, subln_ref, q_ref, kc_ref, vc_ref, k_ref, v_ref, o_ref, qz_ref, m_ref, acc_ref = refs[:11]
        tk = v_ref.shape[2]
    else:
        lamp_ref, subln_ref, q_ref, kc_ref, vc_ref, o_ref, qz_ref, m_ref, acc_ref = refs[:9]
        tk = 0
    s_refs, p_refs, cm_refs = refs[-6:-4], refs[-4:-2], refs[-2:]
    tc = kc_ref.shape[0]
    n_chunks = 1 + n_lat

    _attn_fill_qz(q_ref, qz_ref, tq)
    m_ref[...] = jnp.full(m_ref.shape, -jnp.inf, F32)
    acc_ref[...] = jnp.zeros(acc_ref.shape, F32)

    def scores(k_blk, slot):
        s = lax.dot_general(k_blk, qz_ref[...], _NT, preferred_element_type=F32)
        s_refs[slot][0:k_blk.shape[0], :] = s
        cm_refs[slot][...] = jnp.max(s, axis=0, keepdims=True)

    def values(c, slot):
        if isinstance(c, int) and c == 0:
            return jnp.dot(vc_ref[0], p_refs[slot][0:tc, :], preferred_element_type=F32)
        return jnp.dot(v_ref[c - 1], p_refs[slot][0:tk, :], preferred_element_type=F32)

    def iteration(c, slot, *, first, last):
        if not last:
            scores(k_ref[pl.ds(pl.multiple_of(c * tk, tk), tk), :], 1 - slot)
        pv = None if first else values(c - 1, 1 - slot)
        n = tc if first else tk
        m_old = m_ref[...]
        m_new = jnp.maximum(m_old, cm_refs[slot][...])
        p_refs[slot][0:n, :] = jnp.exp2(s_refs[slot][0:n, :] - m_new).astype(BF16)
        m_ref[...] = m_new
        if not first:
            acc_ref[...] = jnp.exp2(m_old - m_new) * (acc_ref[...] + pv)

    scores(kc_ref[...], 0)
    for c in range(min(2, n_chunks)):
        iteration(c, c % 2, first=c == 0, last=c == n_chunks - 1)
    steady = list(range(2, n_chunks - 1))
    if len(steady) >= 2:
        def pair(j, carry):
            c = 2 + 2 * j
            iteration(c, 0, first=False, last=False)
            iteration(c + 1, 1, first=False, last=False)
            return carry
        lax.fori_loop(0, len(steady) // 2, pair, 0)
    if len(steady) % 2:
        iteration(steady[-1], steady[-1] % 2, first=False, last=False)
    if n_chunks > 2:
        iteration(n_chunks - 1, (n_chunks - 1) % 2, first=False, last=True)
    acc = acc_ref[...] + values(n_chunks - 1, (n_chunks - 1) % 2)
    _attn_finish(acc, lamp_ref, subln_ref, o_ref, tq, lam_init)


def _attn_finish(acc, lamp_ref, subln_ref, o_ref, tq, lam_init):
    lp = lamp_ref[...]
    lam = (jnp.exp(jnp.sum(lp[0:1] * lp[1:2], axis=1, keepdims=True))
           - jnp.exp(jnp.sum(lp[2:3] * lp[3:4], axis=1, keepdims=True)) + lam_init)
    o = acc[:HEAD_W] / acc[HEAD_W:HEAD_W + 1]
    d = o[:, :tq] - lam * o[:, tq:]
    d = d * lax.rsqrt(jnp.mean(d * d, axis=0, keepdims=True) + EPS)
    o_ref[...] = (d.T * subln_ref[...] * (1.0 - lam_init)).astype(o_ref.dtype)


def _attn_bounded_kernel(lamp_ref, subln_ref, q_ref, kc_ref, vc_ref, k_ref, v_ref, o_ref,
                         qz_ref, acc_ref, p0_ref, p1_ref, *, tq, lam_init):
    n_lat, _, tk = v_ref.shape
    tc = kc_ref.shape[0]
    n_chunks = 1 + n_lat
    p_refs = (p0_ref, p1_ref)
    _attn_fill_qz(q_ref, qz_ref, tq)

    def probs(k_blk, slot):
        s = lax.dot_general(k_blk, qz_ref[...], _NT, preferred_element_type=F32)
        p_refs[slot][0:k_blk.shape[0], :] = jnp.exp2(s).astype(BF16)

    def iteration(c, slot, *, first, last):
        if not last:
            probs(k_ref[pl.ds(pl.multiple_of(c * tk, tk), tk), :], 1 - slot)
        if first:
            acc_ref[...] = jnp.dot(vc_ref[0], p_refs[slot][0:tc, :], preferred_element_type=F32)
        else:
            acc_ref[...] += jnp.dot(v_ref[c - 1], p_refs[slot][0:tk, :], preferred_element_type=F32)

    probs(kc_ref[...], 0)
    for c in range(n_chunks):
        iteration(c, c % 2, first=c == 0, last=c == n_chunks - 1)
    _attn_finish(acc_ref[...], lamp_ref, subln_ref, o_ref, tq, lam_init)


def _attn_fill_qz(q_ref, qz_ref, tq):
    q = q_ref[...]
    lane = lax.broadcasted_iota(jnp.int32, q.shape, 1)
    qz_ref[0:tq, :] = jnp.where(lane < QK_DIM, q, jnp.zeros_like(q))
    qz_ref[tq:2 * tq, :] = jnp.where(lane >= QK_DIM, q, jnp.zeros_like(q))


def _attention(lamp, subln, q_arr, q_col0, kc_arr, kc_col0, vct, k_arr=None, k_col0=0, vt=None, score_bound=None,
               *, lam_init, tq=512):
    Lq = q_arr.shape[0]
    C = kc_arr.shape[0]
    heads = vct.shape[1] // V_ROWS
    assert vct.shape == (1, heads * V_ROWS, C)
    hw = heads * HEAD_W
    tq = _blk(Lq, tq)
    n_lat, tk = (0, 0) if k_arr is None else (vt.shape[0], vt.shape[2])
    head_specs = [pl.BlockSpec((4, QK_DIM), lambda h, i: (0, 0)), pl.BlockSpec((1, HEAD_W), lambda h, i: (0, 0))]
    head_args = [lamp, subln.reshape(1, HEAD_W)]
    vc_spec = pl.BlockSpec((1, V_ROWS, C), lambda h, i: (0, h, 0))
    v_spec = pl.BlockSpec((n_lat, V_ROWS, tk), lambda h, i: (0, h, 0))
    in_specs = head_specs + [
        pl.BlockSpec((tq, HEAD_W), lambda h, i: (i, q_col0 + h)),
        pl.BlockSpec((C, HEAD_W), lambda h, i: (0, kc_col0 + h)),
        vc_spec,
    ]
    args = head_args + [q_arr, kc_arr, vct]
    if n_lat:
        assert k_arr.shape[0] == n_lat * tk
        in_specs += [pl.BlockSpec((k_arr.shape[0], HEAD_W), lambda h, i: (0, k_col0 + h)), v_spec]
        args += [k_arr, vt]
    tmax = max(C, tk)
    per_slot = lambda shape, dtype: [pltpu.VMEM(shape, dtype)] * 2
    common = dict(
        grid=(heads, Lq // tq),
        out_specs=pl.BlockSpec((tq, HEAD_W), lambda h, i: (i, h)),
        out_shape=jax.ShapeDtypeStruct((Lq, hw), BF16),
        compiler_params=_params(("arbitrary", "arbitrary"), 56),
    )
    general = pl.pallas_call(
        functools.partial(_attn_kernel, tq=tq, n_lat=n_lat, lam_init=lam_init),
        in_specs=in_specs,
        scratch_shapes=[
            pltpu.VMEM((2 * tq, HEAD_W), BF16),
            pltpu.VMEM((1, 2 * tq), F32),
            pltpu.VMEM((V_ROWS, 2 * tq), F32),
            *per_slot((tmax, 2 * tq), F32),
            *per_slot((tmax, 2 * tq), BF16),
            *per_slot((1, 2 * tq), F32),
        ],
        name="diff_attn",
        **common,
    )
    if score_bound is None:
        return general(*args)
    bounded = pl.pallas_call(
        functools.partial(_attn_bounded_kernel, tq=tq, lam_init=lam_init),
        in_specs=in_specs,
        scratch_shapes=[
            pltpu.VMEM((2 * tq, HEAD_W), BF16),
            pltpu.VMEM((V_ROWS, 2 * tq), F32),
            *per_slot((tmax, 2 * tq), BF16),
        ],
        name="diff_attn_bounded",
        **common,
    )
    return lax.cond(score_bound <= MAX_SCORE_BOUND, bounded, general, *args)


def _mixer_kernel(u_ref, up_ref, un_ref, pw_ref, ps_ref, dw_ref, db_ref, lg_ref, lb_ref, cw_ref, o_ref,
                  pool_ref, glu_ref, *, seq_len):
    bm = u_ref.shape[0]
    pw_cols = pool_ref.shape[1]
    grp = pw_cols // len(POOL_WINDOWS)
    i = pl.program_id(0)
    has_prev = (i > 0).astype(F32)
    has_next = (i < pl.num_programs(0) - 1).astype(F32)

    def fill(row0, src, valid):
        u = src.astype(F32)
        pool_ref[row0:row0 + src.shape[0], :] = u[:, :pw_cols] * valid
        a, gate = u[:, pw_cols:2 * pw_cols], u[:, 2 * pw_cols:]
        glu_ref[row0:row0 + src.shape[0], :] = a * jax.nn.sigmoid(gate) * valid

    fill(0, up_ref[...], has_prev)
    fill(HALO, u_ref[...], 1.0)
    fill(HALO + bm, un_ref[...], has_next)

    t = i * bm + lax.broadcasted_iota(jnp.int32, (bm, 1), 0)
    for g, w in enumerate(POOL_WINDOWS):
        cols = slice(g * grp, (g + 1) * grp)
        tot = pool_ref[HALO - w // 2:HALO - w // 2 + bm, cols]
        for dlt in range(-w // 2 + 1, w - w // 2):
            tot = tot + pool_ref[HALO + dlt:HALO + dlt + bm, cols]
        cnt = (jnp.clip(t + (w - w // 2), 0, seq_len) - jnp.clip(t - w // 2, 0, seq_len)).astype(F32)
        pooled = tot / cnt - pool_ref[HALO:HALO + bm, cols]
        y = jnp.dot(pooled.astype(BF16), pw_ref[g], preferred_element_type=F32) * ps_ref[:, cols]
        o_ref[:, cols] = y.astype(o_ref.dtype)

    half = CONV_K // 2
    y = glu_ref[HALO - half:HALO - half + bm, :] * dw_ref[0:1, :]
    for k in range(1, CONV_K):
        y = y + glu_ref[HALO - half + k:HALO - half + k + bm, :] * dw_ref[k:k + 1, :]
    y = y + db_ref[...]
    mu = jnp.mean(y, axis=-1, keepdims=True)
    yc = y - mu
    var = jnp.mean(yc * yc, axis=-1, keepdims=True)
    z = yc * lax.rsqrt(var + EPS) * lg_ref[...] + lb_ref[...]
    z = jax.nn.silu(z)
    o_ref[:, pw_cols:] = jnp.dot(z.astype(BF16), cw_ref[...], preferred_element_type=F32).astype(o_ref.dtype)


def _mixer(upc, pool_w, pool_scale, conv_dw, conv_db, ln_g, ln_b, conv_pw, *, bm=256):
    L, w3 = upc.shape
    cw = w3 // 3
    bm = _blk(L, bm)
    nb, hb = L // bm, bm // HALO
    last_halo = L // HALO - 1
    dw_pad = jnp.zeros((32, cw), F32).at[:CONV_K].set(conv_dw)
    row = pl.BlockSpec((1, cw), lambda i: (0, 0))
    return pl.pallas_call(
        functools.partial(_mixer_kernel, seq_len=L),
        grid=(nb,),
        in_specs=[
            pl.BlockSpec((bm, w3), lambda i: (i, 0)),
            pl.BlockSpec((HALO, w3), lambda i: (jnp.maximum(i * hb - 1, 0), 0)),
            pl.BlockSpec((HALO, w3), lambda i: (jnp.minimum((i + 1) * hb, last_halo), 0)),
            pl.BlockSpec(pool_w.shape, lambda i: (0, 0, 0)),
            row,
            pl.BlockSpec((32, cw), lambda i: (0, 0)),
            row, row, row,
            pl.BlockSpec((cw, cw), lambda i: (0, 0)),
        ],
        out_specs=pl.BlockSpec((bm, 2 * cw), lambda i: (i, 0)),
        out_shape=jax.ShapeDtypeStruct((L, 2 * cw), BF16),
        scratch_shapes=[pltpu.VMEM((bm + 2 * HALO, cw), F32), pltpu.VMEM((bm + 2 * HALO, cw), F32)],
        compiler_params=_params(("arbitrary",), 40),
        name="pool_conv_mixer",
    )(upc, upc, upc, pool_w.astype(BF16), pool_scale.reshape(1, cw), dw_pad, conv_db.reshape(1, cw),
      ln_g.reshape(1, cw), ln_b.reshape(1, cw), conv_pw.astype(BF16))


def _rope_tables(n_tok):
    r = jnp.repeat(jnp.arange(n_tok // GRID_W), GRID_W).astype(F32)
    col = jnp.tile(jnp.arange(GRID_W), n_tok // GRID_W).astype(F32)
    n_freq = QK_DIM // 4
    inv = ROPE_BASE ** (-jnp.arange(n_freq, dtype=F32) / n_freq)
    ang = jnp.concatenate([r[:, None] * inv, col[:, None] * inv], axis=-1)
    cos, sin = jnp.cos(ang), jnp.sin(ang)
    reps = LANES // QK_DIM
    return jnp.tile(jnp.concatenate([cos, cos], -1), (1, reps)), jnp.tile(jnp.concatenate([-sin, sin], -1), (1, reps))


def _forward(x, c, ctx, c_ctx, w_ada, b_ada, norm1, norm2, w_in, w_out, q_norm, k_norm,
             lambda_q1, lambda_k1, lambda_q2, lambda_k2, subln, pool_w, pool_scale,
             conv_dw, conv_db, conv_ln_g, conv_ln_b, conv_pw, w_mlp1, w_mlp2, *, heads, tk):
    depth = w_in.shape[0]
    S, D = x.shape[1], x.shape[2]
    qw = heads * HEAD_W
    kv_off, pc_off = qw, 3 * qw
    pc_w = w_in.shape[2] - pc_off
    hidden = w_mlp1.shape[2]
    qblk = qw // HEAD_W

    xl = x[0]
    xc = ctx[0].astype(x.dtype)
    cos, sin = _rope_tables(S)
    gidx = jnp.arange(MXU_W) // QK_DIM
    gmat = (gidx[:, None] == gidx[None, :]).astype(BF16)

    cond8 = jnp.zeros((8, D), F32).at[0].set(c[0]).at[1].set(c_ctx)
    mods = _ada(cond8, w_ada, b_ada)

    for l in range(depth):
        last = l == depth - 1
        lam_init = 0.8 - 0.6 * math.exp(-0.3 * l)
        sh1, sc1, g1, sh2, sc2, g2 = jnp.split(mods[l, 0], N_MOD)
        csh1, csc1, cg1, csh2, csc2, cg2 = jnp.split(mods[l, 1], N_MOD)
        w_in_l, w_out_l = w_in[l].astype(BF16), w_out[l].astype(BF16)
        w1_l, w2_l = w_mlp1[l].astype(BF16), w_mlp2[l].astype(BF16)
        reps = qw // QK_DIM
        gain = jnp.concatenate([jnp.tile(q_norm[l], reps) * QK_SCALE,
                                jnp.tile(k_norm[l], reps)])[None, :]
        lamp = jnp.stack([lambda_q1[l], lambda_k1[l], lambda_q2[l], lambda_k2[l]]).astype(F32)
        mixer_w = (pool_w[l], pool_scale[l], conv_dw[l], conv_db[l], conv_ln_g[l], conv_ln_b[l], conv_pw[l])

        h = _prenorm(xl, norm1[l], sc1, sh1)
        hc = _prenorm(xc, norm1[l], csc1, csh1)
        qk = _mm_qk(h, w_in_l, gain, gmat, cos, sin)
        vt = _mm_vt(h, w_in_l, 2 * qw, qw, tk)
        upc = _mm_cast(h, w_in_l, pc_off, pc_w)
        qkc = _mm_qk(hc, w_in_l, gain, gmat)
        vct = _mm_vt(hc, w_in_l, 2 * qw, qw, xc.shape[0])

        score_bound = ((QK_DIM * QK_SCALE * BOUND_SLACK)
                       * jnp.max(jnp.abs(q_norm[l])) * jnp.max(jnp.abs(k_norm[l]))).astype(F32)
        attn = _attention(lamp, subln[l], qk, 0, qkc, qblk, vct, qk, qblk, vt, score_bound, lam_init=lam_init)
        pc = _mixer(upc, *mixer_w)
        x_new = _mm_resid([attn, pc], w_out_l, xl, g1, bk=qw)
        h2 = _prenorm(x_new, norm2[l], sc2, sh2)
        hid = _mm_cast(h2, w1_l, 0, hidden, relu2=True)
        x_new = _mm_resid([hid], w2_l, x_new, g2, bk=hidden, bm=512, bn=256)

        if not last:
            attn_c = _attention(lamp, subln[l], qkc, 0, qkc, qblk, vct, lam_init=lam_init)
            upcc = _mm_cast(hc, w_in_l, pc_off, pc_w)
            pcc = _mixer(upcc, *mixer_w)
            xc_new = _mm_resid([attn_c, pcc], w_out_l, xc, cg1, bk=qw)
            hc2 = _prenorm(xc_new, norm2[l], csc2, csh2)
            hidc = _mm_cast(hc2, w1_l, 0, hidden, relu2=True)
            xc = _mm_resid([hidc], w2_l, xc_new, cg2, bk=hidden, bm=512, bn=256)
        xl = x_new
    return xl[None]


def kernel(x, c, ctx, c_ctx, w_ada, b_ada, norm1, norm2, w_in, w_out, q_norm, k_norm, lambda_q1, lambda_k1,
           lambda_q2, lambda_k2, subln, pool_w, pool_scale, conv_dw, conv_db, conv_ln_g, conv_ln_b, conv_pw,
           w_mlp1, w_mlp2):
    return _forward(x, c, ctx, c_ctx, w_ada, b_ada, norm1, norm2, w_in, w_out, q_norm, k_norm,
                    lambda_q1, lambda_k1, lambda_q2, lambda_k2, subln, pool_w, pool_scale,
                    conv_dw, conv_db, conv_ln_g, conv_ln_b, conv_pw, w_mlp1, w_mlp2,
                    heads=ATTN_HEADS, tk=1024)
```

```python
import functools
import math

import jax
import jax.numpy as jnp
from jax import lax
from jax.experimental import pallas as pl
from jax.experimental.pallas import tpu as pltpu

F32 = jnp.float32
BF16 = jnp.bfloat16

ATTN_HEADS = 16
QK_DIM = 64
HEAD_W = 2 * QK_DIM
V_ROWS = HEAD_W + 16
GRID_W = 64
N_MOD = 6
POOL_WINDOWS = (2, 4, 8, 16)
CONV_K = 31
HALO = 16
ROPE_BASE = 10000.0
EPS = 1e-6
QK_SCALE = QK_DIM ** -0.5 * math.log2(math.e)
BOUND_SLACK = 1.05
MAX_SCORE_BOUND = 60.0

LANES = 128
MXU_W = 256
VMEM_BYTES = 64 << 20


def _params(sem, vmem_mb):
    assert (vmem_mb << 20) <= VMEM_BYTES
    return pltpu.CompilerParams(dimension_semantics=sem, vmem_limit_bytes=vmem_mb << 20)


def _blk(n, pref):
    if n <= pref:
        return n
    b = pref
    while n % b:
        b //= 2
    return b


def _ada_kernel(cond_ref, w_ref, b_ref, o_ref):
    s = jax.nn.silu(cond_ref[...]).astype(BF16)
    o_ref[...] = jnp.dot(s, w_ref[...].astype(BF16), preferred_element_type=F32) + b_ref[...]


def _ada(cond8, w_ada, b_ada):
    depth, d, n = w_ada.shape
    bn = _blk(n, 512)
    return pl.pallas_call(
        _ada_kernel,
        grid=(depth, n // bn),
        in_specs=[
            pl.BlockSpec((8, d), lambda l, j: (0, 0)),
            pl.BlockSpec((None, d, bn), lambda l, j: (l, 0, j)),
            pl.BlockSpec((None, 1, bn), lambda l, j: (l, 0, j)),
        ],
        out_specs=pl.BlockSpec((None, 8, bn), lambda l, j: (l, 0, j)),
        out_shape=jax.ShapeDtypeStruct((depth, 8, n), F32),
        compiler_params=_params(("arbitrary", "arbitrary"), 40),
        name="ada_mod",
    )(cond8, w_ada, b_ada.reshape(depth, 1, n))


def _prenorm_kernel(x_ref, g_ref, sc_ref, sh_ref, o_ref):
    x = x_ref[...]
    y = x * lax.rsqrt(jnp.mean(x * x, axis=-1, keepdims=True) + EPS) * g_ref[...]
    o_ref[...] = (y * (1.0 + sc_ref[...]) + sh_ref[...]).astype(o_ref.dtype)


def _prenorm(x, g, scale, shift):
    L, d = x.shape
    bm = _blk(L, 512)
    row = pl.BlockSpec((1, d), lambda i: (0, 0))
    return pl.pallas_call(
        _prenorm_kernel,
        grid=(L // bm,),
        in_specs=[pl.BlockSpec((bm, d), lambda i: (i, 0)), row, row, row],
        out_specs=pl.BlockSpec((bm, d), lambda i: (i, 0)),
        out_shape=jax.ShapeDtypeStruct((L, d), BF16),
        compiler_params=_params(("arbitrary",), 40),
        name="prenorm",
    )(x, g.reshape(1, d), scale.reshape(1, d), shift.reshape(1, d))


def _resident_bf16(w_ref, wb_ref):
    @pl.when(pl.program_id(1) == 0)
    def _():
        wb_ref[...] = w_ref[...].astype(BF16)
    return wb_ref[...]


def _mm_cast_kernel(a_ref, w_ref, o_ref, wb_ref, *, relu2):
    acc = jnp.dot(a_ref[...], _resident_bf16(w_ref, wb_ref), preferred_element_type=F32)
    if relu2:
        acc = jnp.square(jnp.maximum(acc, 0.0))
    o_ref[...] = acc.astype(o_ref.dtype)


def _mm_cast(a, w, col_off, n, *, relu2=False, bm=1024, bn=512):
    M, K = a.shape
    bm, bn = _blk(M, bm), _blk(n, bn)
    assert col_off % bn == 0
    joff = col_off // bn
    return pl.pallas_call(
        functools.partial(_mm_cast_kernel, relu2=relu2),
        grid=(n // bn, M // bm),
        in_specs=[
            pl.BlockSpec((bm, K), lambda j, i: (i, 0)),
            pl.BlockSpec((K, bn), lambda j, i: (0, j + joff)),
        ],
        out_specs=pl.BlockSpec((bm, bn), lambda j, i: (i, j)),
        out_shape=jax.ShapeDtypeStruct((M, n), BF16),
        scratch_shapes=[pltpu.VMEM((K, bn), BF16)],
        compiler_params=_params(("arbitrary", "arbitrary"), 48),
        name="proj_relu2" if relu2 else "proj_cast",
    )(a, w)


def _mm_vt_kernel(a_ref, w_ref, o_ref, wb_ref, *, tk):
    acc = jnp.dot(a_ref[...], _resident_bf16(w_ref, wb_ref), preferred_element_type=F32)
    ones = jnp.ones((V_ROWS - HEAD_W, tk), o_ref.dtype)
    for c in range(a_ref.shape[0] // tk):
        for h in range(acc.shape[1] // HEAD_W):
            vt = acc[c * tk:(c + 1) * tk, h * HEAD_W:(h + 1) * HEAD_W].T
            o_ref[c, h * V_ROWS:h * V_ROWS + HEAD_W, :] = vt.astype(o_ref.dtype)
            o_ref[c, h * V_ROWS + HEAD_W:(h + 1) * V_ROWS, :] = ones


def _mm_vt(a, w, col_off, n, tk, *, bm=1024, bn=512):
    M, K = a.shape
    bm, bn = _blk(M, bm), _blk(n, bn)
    assert bm % tk == 0 and col_off % bn == 0 and bn % HEAD_W == 0
    joff = col_off // bn
    bn_out, n_out = bn // HEAD_W * V_ROWS, n // HEAD_W * V_ROWS
    return pl.pallas_call(
        functools.partial(_mm_vt_kernel, tk=tk),
        grid=(n // bn, M // bm),
        in_specs=[
            pl.BlockSpec((bm, K), lambda j, i: (i, 0)),
            pl.BlockSpec((K, bn), lambda j, i: (0, j + joff)),
        ],
        out_specs=pl.BlockSpec((bm // tk, bn_out, tk), lambda j, i: (i, j, 0)),
        out_shape=jax.ShapeDtypeStruct((M // tk, n_out, tk), BF16),
        scratch_shapes=[pltpu.VMEM((K, bn), BF16)],
        compiler_params=_params(("arbitrary", "arbitrary"), 48),
        name="proj_vt",
    )(a, w)


def _mm_qk_kernel(a_ref, w_ref, gain_ref, gmat_ref, *rest, rope):
    if rope:
        cos_ref, sin_ref, o_ref, wb_ref = rest
    else:
        o_ref, wb_ref = rest
    acc = jnp.dot(a_ref[...], _resident_bf16(w_ref, wb_ref), preferred_element_type=F32)
    bm, bn = acc.shape
    if rope:
        cos, sin = cos_ref[...], sin_ref[...]
        low_half = (lax.broadcasted_iota(jnp.int32, (bm, LANES), 1) & (QK_DIM // 2)) == 0
    for c in range(bn // MXU_W):
        t = acc[:, c * MXU_W:(c + 1) * MXU_W]
        ss = jnp.dot((t * t).astype(BF16), gmat_ref[...], preferred_element_type=F32)
        y = t * lax.rsqrt(ss * (1.0 / QK_DIM) + EPS) * gain_ref[:, c * MXU_W:(c + 1) * MXU_W]
        for h in range(MXU_W // LANES):
            yy = y[:, h * LANES:(h + 1) * LANES]
            if rope:
                partner = jnp.where(low_half, pltpu.roll(yy, LANES - QK_DIM // 2, 1), pltpu.roll(yy, QK_DIM // 2, 1))
                yy = yy * cos + partner * sin
            lo = c * MXU_W + h * LANES
            o_ref[:, lo:lo + LANES] = yy.astype(o_ref.dtype)


def _mm_qk(a, w, gain_row, gmat, cos=None, sin=None, *, bm=1024, bn=512):
    M, K = a.shape
    n = gain_row.shape[1]
    bm, bn = _blk(M, bm), _blk(n, bn)
    rope = cos is not None
    in_specs = [
        pl.BlockSpec((bm, K), lambda j, i: (i, 0)),
        pl.BlockSpec((K, bn), lambda j, i: (0, j)),
        pl.BlockSpec((1, bn), lambda j, i: (0, j)),
        pl.BlockSpec((MXU_W, MXU_W), lambda j, i: (0, 0)),
    ]
    args = [a, w, gain_row, gmat]
    if rope:
        in_specs += [pl.BlockSpec((bm, LANES), lambda j, i: (i, 0))] * 2
        args += [cos, sin]
    return pl.pallas_call(
        functools.partial(_mm_qk_kernel, rope=rope),
        grid=(n // bn, M // bm),
        in_specs=in_specs,
        out_specs=pl.BlockSpec((bm, bn), lambda j, i: (i, j)),
        out_shape=jax.ShapeDtypeStruct((M, n), BF16),
        scratch_shapes=[pltpu.VMEM((K, bn), BF16)],
        compiler_params=_params(("arbitrary", "arbitrary"), 52),
        name="proj_qk",
    )(*args)


def _mm_resid_kernel(*refs, n_lhs):
    a_refs, w_refs = refs[:n_lhs], refs[n_lhs:2 * n_lhs]
    x_ref, g_ref, o_ref, acc_ref = refs[2 * n_lhs:]
    k = pl.program_id(2)
    part = jnp.dot(a_refs[0][...], w_refs[0][...], preferred_element_type=F32)
    for a_ref, w_ref in zip(a_refs[1:], w_refs[1:]):
        part += jnp.dot(a_ref[...], w_ref[...], preferred_element_type=F32)

    @pl.when(k == 0)
    def _():
        acc_ref[...] = part

    @pl.when(k > 0)
    def _():
        acc_ref[...] += part

    @pl.when(k == pl.num_programs(2) - 1)
    def _():
        o_ref[...] = x_ref[...] + g_ref[...] * acc_ref[...]


def _mm_resid(lhs, w, x, gate, *, bk, bm=1024, bn=512):
    M, n = x.shape
    kw = lhs[0].shape[1]
    assert all(a.shape == (M, kw) for a in lhs) and w.shape == (kw * len(lhs), n) and kw % bk == 0
    bm, bn = _blk(M, bm), _blk(n, bn)
    ksteps = kw // bk
    in_specs = [pl.BlockSpec((bm, bk), lambda i, j, k: (i, k)) for _ in lhs]
    in_specs += [pl.BlockSpec((bk, bn), functools.partial(lambda i, j, k, s: (s * ksteps + k, j), s=s))
                 for s in range(len(lhs))]
    in_specs += [pl.BlockSpec((bm, bn), lambda i, j, k: (i, j)), pl.BlockSpec((1, bn), lambda i, j, k: (0, j))]
    return pl.pallas_call(
        functools.partial(_mm_resid_kernel, n_lhs=len(lhs)),
        grid=(M // bm, n // bn, ksteps),
        in_specs=in_specs,
        out_specs=pl.BlockSpec((bm, bn), lambda i, j, k: (i, j)),
        out_shape=jax.ShapeDtypeStruct((M, n), F32),
        scratch_shapes=[pltpu.VMEM((bm, bn), F32)],
        compiler_params=_params(("arbitrary", "arbitrary", "arbitrary"), 56),
        name="proj_resid",
    )(*lhs, *([w] * len(lhs)), x, gate.reshape(1, n))


_NT = (((1,), (1,)), ((), ()))


def _attn_kernel(*refs, tq, n_lat, lam_init):
    if n_lat:
        lamp_ref, subln_ref, q_ref, kc_ref, vc_ref, k_ref, v_ref, o_ref, qz_ref, m_ref, acc_ref = refs[:11]
        tk = v_ref.shape[2]
    else:
        lamp_ref, subln_ref, q_ref, kc_ref, vc_ref, o_ref, qz_ref, m_ref, acc_ref = refs[:9]
        tk = 0
    s_refs, p_refs, cm_refs = refs[-6:-4], refs[-4:-2], refs[-2:]
    tc = kc_ref.shape[0]
    n_chunks = 1 + n_lat

    _attn_fill_qz(q_ref, qz_ref, tq)
    m_ref[...] = jnp.full(m_ref.shape, -jnp.inf, F32)
    acc_ref[...] = jnp.zeros(acc_ref.shape, F32)

    def scores(k_blk, slot):
        s = lax.dot_general(k_blk, qz_ref[...], _NT, preferred_element_type=F32)
        s_refs[slot][0:k_blk.shape[0], :] = s
        cm_refs[slot][...] = jnp.max(s, axis=0, keepdims=True)

    def values(c, slot):
        if isinstance(c, int) and c == 0:
            return jnp.dot(vc_ref[0], p_refs[slot][0:tc, :], preferred_element_type=F32)
        return jnp.dot(v_ref[c - 1], p_refs[slot][0:tk, :], preferred_element_type=F32)

    def iteration(c, slot, *, first, last):
        if not last:
            scores(k_ref[pl.ds(pl.multiple_of(c * tk, tk), tk), :], 1 - slot)
        pv = None if first else values(c - 1, 1 - slot)
        n = tc if first else tk
        m_old = m_ref[...]
        m_new = jnp.maximum(m_old, cm_refs[slot][...])
        p_refs[slot][0:n, :] = jnp.exp2(s_refs[slot][0:n, :] - m_new).astype(BF16)
        m_ref[...] = m_new
        if not first:
            acc_ref[...] = jnp.exp2(m_old - m_new) * (acc_ref[...] + pv)

    scores(kc_ref[...], 0)
    for c in range(min(2, n_chunks)):
        iteration(c, c % 2, first=c == 0, last=c == n_chunks - 1)
    steady = list(range(2, n_chunks - 1))
    if len(steady) >= 2:
        def pair(j, carry):
            c = 2 + 2 * j
            iteration(c, 0, first=False, last=False)
            iteration(c + 1, 1, first=False, last=False)
            return carry
        lax.fori_loop(0, len(steady) // 2, pair, 0)
    if len(steady) % 2:
        iteration(steady[-1], steady[-1] % 2, first=False, last=False)
    if n_chunks > 2:
        iteration(n_chunks - 1, (n_chunks - 1) % 2, first=False, last=True)
    acc = acc_ref[...] + values(n_chunks - 1, (n_chunks - 1) % 2)
    _attn_finish(acc, lamp_ref, subln_ref, o_ref, tq, lam_init)


def _attn_finish(acc, lamp_ref, subln_ref, o_ref, tq, lam_init):
    lp = lamp_ref[...]
    lam = (jnp.exp(jnp.sum(lp[0:1] * lp[1:2], axis=1, keepdims=True))
           - jnp.exp(jnp.sum(lp[2:3] * lp[3:4], axis=1, keepdims=True)) + lam_init)
    o = acc[:HEAD_W] / acc[HEAD_W:HEAD_W + 1]
    d = o[:, :tq] - lam * o[:, tq:]
    d = d * lax.rsqrt(jnp.mean(d * d, axis=0, keepdims=True) + EPS)
    o_ref[...] = (d.T * subln_ref[...] * (1.0 - lam_init)).astype(o_ref.dtype)


def _attn_bounded_kernel(lamp_ref, subln_ref, q_ref, kc_ref, vc_ref, k_ref, v_ref, o_ref,
                         qz_ref, acc_ref, p0_ref, p1_ref, *, tq, lam_init):
    n_lat, _, tk = v_ref.shape
    tc = kc_ref.shape[0]
    n_chunks = 1 + n_lat
    p_refs = (p0_ref, p1_ref)
    _attn_fill_qz(q_ref, qz_ref, tq)

    def probs(k_blk, slot):
        s = lax.dot_general(k_blk, qz_ref[...], _NT, preferred_element_type=F32)
        p_refs[slot][0:k_blk.shape[0], :] = jnp.exp2(s).astype(BF16)

    def iteration(c, slot, *, first, last):
        if not last:
            probs(k_ref[pl.ds(pl.multiple_of(c * tk, tk), tk), :], 1 - slot)
        if first:
            acc_ref[...] = jnp.dot(vc_ref[0], p_refs[slot][0:tc, :], preferred_element_type=F32)
        else:
            acc_ref[...] += jnp.dot(v_ref[c - 1], p_refs[slot][0:tk, :], preferred_element_type=F32)

    probs(kc_ref[...], 0)
    for c in range(n_chunks):
        iteration(c, c % 2, first=c == 0, last=c == n_chunks - 1)
    _attn_finish(acc_ref[...], lamp_ref, subln_ref, o_ref, tq, lam_init)


def _attn_fill_qz(q_ref, qz_ref, tq):
    q = q_ref[...]
    lane = lax.broadcasted_iota(jnp.int32, q.shape, 1)
    qz_ref[0:tq, :] = jnp.where(lane < QK_DIM, q, jnp.zeros_like(q))
    qz_ref[tq:2 * tq, :] = jnp.where(lane >= QK_DIM, q, jnp.zeros_like(q))


def _attention(lamp, subln, q_arr, q_col0, kc_arr, kc_col0, vct, k_arr=None, k_col0=0, vt=None, score_bound=None,
               *, lam_init, tq=512):
    Lq = q_arr.shape[0]
    C = kc_arr.shape[0]
    heads = vct.shape[1] // V_ROWS
    assert vct.shape == (1, heads * V_ROWS, C)
    hw = heads * HEAD_W
    tq = _blk(Lq, tq)
    n_lat, tk = (0, 0) if k_arr is None else (vt.shape[0], vt.shape[2])
    head_specs = [pl.BlockSpec((4, QK_DIM), lambda h, i: (0, 0)), pl.BlockSpec((1, HEAD_W), lambda h, i: (0, 0))]
    head_args = [lamp, subln.reshape(1, HEAD_W)]
    vc_spec = pl.BlockSpec((1, V_ROWS, C), lambda h, i: (0, h, 0))
    v_spec = pl.BlockSpec((n_lat, V_ROWS, tk), lambda h, i: (0, h, 0))
    in_specs = head_specs + [
        pl.BlockSpec((tq, HEAD_W), lambda h, i: (i, q_col0 + h)),
        pl.BlockSpec((C, HEAD_W), lambda h, i: (0, kc_col0 + h)),
        vc_spec,
    ]
    args = head_args + [q_arr, kc_arr, vct]
    if n_lat:
        assert k_arr.shape[0] == n_lat * tk
        in_specs += [pl.BlockSpec((k_arr.shape[0], HEAD_W), lambda h, i: (0, k_col0 + h)), v_spec]
        args += [k_arr, vt]
    tmax = max(C, tk)
    per_slot = lambda shape, dtype: [pltpu.VMEM(shape, dtype)] * 2
    common = dict(
        grid=(heads, Lq // tq),
        out_specs=pl.BlockSpec((tq, HEAD_W), lambda h, i: (i, h)),
        out_shape=jax.ShapeDtypeStruct((Lq, hw), BF16),
        compiler_params=_params(("arbitrary", "arbitrary"), 56),
    )
    general = pl.pallas_call(
        functools.partial(_attn_kernel, tq=tq, n_lat=n_lat, lam_init=lam_init),
        in_specs=in_specs,
        scratch_shapes=[
            pltpu.VMEM((2 * tq, HEAD_W), BF16),
            pltpu.VMEM((1, 2 * tq), F32),
            pltpu.VMEM((V_ROWS, 2 * tq), F32),
            *per_slot((tmax, 2 * tq), F32),
            *per_slot((tmax, 2 * tq), BF16),
            *per_slot((1, 2 * tq), F32),
        ],
        name="diff_attn",
        **common,
    )
    if score_bound is None:
        return general(*args)
    bounded = pl.pallas_call(
        functools.partial(_attn_bounded_kernel, tq=tq, lam_init=lam_init),
        in_specs=in_specs,
        scratch_shapes=[
            pltpu.VMEM((2 * tq, HEAD_W), BF16),
            pltpu.VMEM((V_ROWS, 2 * tq), F32),
            *per_slot((tmax, 2 * tq), BF16),
        ],
        name="diff_attn_bounded",
        **common,
    )
    return lax.cond(score_bound <= MAX_SCORE_BOUND, bounded, general, *args)


def _mixer_kernel(u_ref, up_ref, un_ref, pw_ref, ps_ref, dw_ref, db_ref, lg_ref, lb_ref, cw_ref, o_ref,
                  pool_ref, glu_ref, *, seq_len):
    bm = u_ref.shape[0]
    pw_cols = pool_ref.shape[1]
    grp = pw_cols // len(POOL_WINDOWS)
    i = pl.program_id(0)
    has_prev = (i > 0).astype(F32)
    has_next = (i < pl.num_programs(0) - 1).astype(F32)

    def fill(row0, src, valid):
        u = src.astype(F32)
        pool_ref[row0:row0 + src.shape[0], :] = u[:, :pw_cols] * valid
        a, gate = u[:, pw_cols:2 * pw_cols], u[:, 2 * pw_cols:]
        glu_ref[row0:row0 + src.shape[0], :] = a * jax.nn.sigmoid(gate) * valid

    fill(0, up_ref[...], has_prev)
    fill(HALO, u_ref[...], 1.0)
    fill(HALO + bm, un_ref[...], has_next)

    t = i * bm + lax.broadcasted_iota(jnp.int32, (bm, 1), 0)
    for g, w in enumerate(POOL_WINDOWS):
        cols = slice(g * grp, (g + 1) * grp)
        tot = pool_ref[HALO - w // 2:HALO - w // 2 + bm, cols]
        for dlt in range(-w // 2 + 1, w - w // 2):
            tot = tot + pool_ref[HALO + dlt:HALO + dlt + bm, cols]
        cnt = (jnp.clip(t + (w - w // 2), 0, seq_len) - jnp.clip(t - w // 2, 0, seq_len)).astype(F32)
        pooled = tot / cnt - pool_ref[HALO:HALO + bm, cols]
        y = jnp.dot(pooled.astype(BF16), pw_ref[g], preferred_element_type=F32) * ps_ref[:, cols]
        o_ref[:, cols] = y.astype(o_ref.dtype)

    first = HALO - CONV_K // 2
    y = None
    for r in range(8):
        z = None
        for m in range((first + CONV_K - 1) // 8 + 1):
            k = 8 * m + r - first
            if 0 <= k < CONV_K:
                term = glu_ref[8 * m:8 * m + bm + 8, :] * dw_ref[k:k + 1, :]
                z = term if z is None else z + term
        z = z[r:r + bm, :]
        y = z if y is None else y + z
    y = y + db_ref[...]
    mu = jnp.mean(y, axis=-1, keepdims=True)
    yc = y - mu
    var = jnp.mean(yc * yc, axis=-1, keepdims=True)
    z = yc * lax.rsqrt(var + EPS) * lg_ref[...] + lb_ref[...]
    z = jax.nn.silu(z)
    o_ref[:, pw_cols:] = jnp.dot(z.astype(BF16), cw_ref[...], preferred_element_type=F32).astype(o_ref.dtype)


def _mixer(upc, pool_w, pool_scale, conv_dw, conv_db, ln_g, ln_b, conv_pw, *, bm=256):
    L, w3 = upc.shape
    cw = w3 // 3
    bm = _blk(L, bm)
    nb, hb = L // bm, bm // HALO
    last_halo = L // HALO - 1
    dw_pad = jnp.zeros((32, cw), F32).at[:CONV_K].set(conv_dw)
    row = pl.BlockSpec((1, cw), lambda i: (0, 0))
    return pl.pallas_call(
        functools.partial(_mixer_kernel, seq_len=L),
        grid=(nb,),
        in_specs=[
            pl.BlockSpec((bm, w3), lambda i: (i, 0)),
            pl.BlockSpec((HALO, w3), lambda i: (jnp.maximum(i * hb - 1, 0), 0)),
            pl.BlockSpec((HALO, w3), lambda i: (jnp.minimum((i + 1) * hb, last_halo), 0)),
            pl.BlockSpec(pool_w.shape, lambda i: (0, 0, 0)),
            row,
            pl.BlockSpec((32, cw), lambda i: (0, 0)),
            row, row, row,
            pl.BlockSpec((cw, cw), lambda i: (0, 0)),
        ],
        out_specs=pl.BlockSpec((bm, 2 * cw), lambda i: (i, 0)),
        out_shape=jax.ShapeDtypeStruct((L, 2 * cw), BF16),
        scratch_shapes=[pltpu.VMEM((bm + 2 * HALO, cw), F32), pltpu.VMEM((bm + 2 * HALO, cw), F32)],
        compiler_params=_params(("arbitrary",), 40),
        name="pool_conv_mixer",
    )(upc, upc, upc, pool_w.astype(BF16), pool_scale.reshape(1, cw), dw_pad, conv_db.reshape(1, cw),
      ln_g.reshape(1, cw), ln_b.reshape(1, cw), conv_pw.astype(BF16))


def _rope_tables(n_tok):
    r = jnp.repeat(jnp.arange(n_tok // GRID_W), GRID_W).astype(F32)
    col = jnp.tile(jnp.arange(GRID_W), n_tok // GRID_W).astype(F32)
    n_freq = QK_DIM // 4
    inv = ROPE_BASE ** (-jnp.arange(n_freq, dtype=F32) / n_freq)
    ang = jnp.concatenate([r[:, None] * inv, col[:, None] * inv], axis=-1)
    cos, sin = jnp.cos(ang), jnp.sin(ang)
    reps = LANES // QK_DIM
    return jnp.tile(jnp.concatenate([cos, cos], -1), (1, reps)), jnp.tile(jnp.concatenate([-sin, sin], -1), (1, reps))


def _forward(x, c, ctx, c_ctx, w_ada, b_ada, norm1, norm2, w_in, w_out, q_norm, k_norm,
             lambda_q1, lambda_k1, lambda_q2, lambda_k2, subln, pool_w, pool_scale,
             conv_dw, conv_db, conv_ln_g, conv_ln_b, conv_pw, w_mlp1, w_mlp2, *, heads, tk):
    depth = w_in.shape[0]
    S, D = x.shape[1], x.shape[2]
    qw = heads * HEAD_W
    kv_off, pc_off = qw, 3 * qw
    pc_w = w_in.shape[2] - pc_off
    hidden = w_mlp1.shape[2]
    qblk = qw // HEAD_W

    xl = x[0]
    xc = ctx[0].astype(x.dtype)
    cos, sin = _rope_tables(S)
    gidx = jnp.arange(MXU_W) // QK_DIM
    gmat = (gidx[:, None] == gidx[None, :]).astype(BF16)

    cond8 = jnp.zeros((8, D), F32).at[0].set(c[0]).at[1].set(c_ctx)
    mods = _ada(cond8, w_ada, b_ada)

    for l in range(depth):
        last = l == depth - 1
        lam_init = 0.8 - 0.6 * math.exp(-0.3 * l)
        sh1, sc1, g1, sh2, sc2, g2 = jnp.split(mods[l, 0], N_MOD)
        csh1, csc1, cg1, csh2, csc2, cg2 = jnp.split(mods[l, 1], N_MOD)
        w_in_l, w1_l = w_in[l], w_mlp1[l]
        w_out_l, w2_l = w_out[l].astype(BF16), w_mlp2[l].astype(BF16)
        reps = qw // QK_DIM
        gain = jnp.concatenate([jnp.tile(q_norm[l], reps) * QK_SCALE,
                                jnp.tile(k_norm[l], reps)])[None, :]
        lamp = jnp.stack([lambda_q1[l], lambda_k1[l], lambda_q2[l], lambda_k2[l]]).astype(F32)
        mixer_w = (pool_w[l], pool_scale[l], conv_dw[l], conv_db[l], conv_ln_g[l], conv_ln_b[l], conv_pw[l])

        h = _prenorm(xl, norm1[l], sc1, sh1)
        hc = _prenorm(xc, norm1[l], csc1, csh1)
        qk = _mm_qk(h, w_in_l, gain, gmat, cos, sin)
        vt = _mm_vt(h, w_in_l, 2 * qw, qw, tk)
        upc = _mm_cast(h, w_in_l, pc_off, pc_w)
        qkc = _mm_qk(hc, w_in_l, gain, gmat)
        vct = _mm_vt(hc, w_in_l, 2 * qw, qw, xc.shape[0])

        score_bound = ((QK_DIM * QK_SCALE * BOUND_SLACK)
                       * jnp.max(jnp.abs(q_norm[l])) * jnp.max(jnp.abs(k_norm[l]))).astype(F32)
        attn = _attention(lamp, subln[l], qk, 0, qkc, qblk, vct, qk, qblk, vt, score_bound, lam_init=lam_init)
        pc = _mixer(upc, *mixer_w)
        x_new = _mm_resid([attn, pc], w_out_l, xl, g1, bk=qw)
        h2 = _prenorm(x_new, norm2[l], sc2, sh2)
        hid = _mm_cast(h2, w1_l, 0, hidden, relu2=True)
        x_new = _mm_resid([hid], w2_l, x_new, g2, bk=hidden, bm=512, bn=256)

        if not last:
            attn_c = _attention(lamp, subln[l], qkc, 0, qkc, qblk, vct, lam_init=lam_init)
            upcc = _mm_cast(hc, w_in_l, pc_off, pc_w)
            pcc = _mixer(upcc, *mixer_w)
            xc_new = _mm_resid([attn_c, pcc], w_out_l, xc, cg1, bk=qw)
            hc2 = _prenorm(xc_new, norm2[l], csc2, csh2)
            hidc = _mm_cast(hc2, w1_l, 0, hidden, relu2=True)
            xc = _mm_resid([hidc], w2_l, xc_new, cg2, bk=hidden, bm=512, bn=256)
        xl = x_new
    return xl[None]


def kernel(x, c, ctx, c_ctx, w_ada, b_ada, norm1, norm2, w_in, w_out, q_norm, k_norm, lambda_q1, lambda_k1,
           lambda_q2, lambda_k2, subln, pool_w, pool_scale, conv_dw, conv_db, conv_ln_g, conv_ln_b, conv_pw,
           w_mlp1, w_mlp2):
    return _forward(x, c, ctx, c_ctx, w_ada, b_ada, norm1, norm2, w_in, w_out, q_norm, k_norm,
                    lambda_q1, lambda_k1, lambda_q2, lambda_k2, subln, pool_w, pool_scale,
                    conv_dw, conv_db, conv_ln_g, conv_ln_b, conv_pw, w_mlp1, w_mlp2,
                    heads=ATTN_HEADS, tk=1024)
```

```python
import functools
import math

import jax
import jax.numpy as jnp
from jax import lax
from jax.experimental import pallas as pl
from jax.experimental.pallas import tpu as pltpu

F32 = jnp.float32
BF16 = jnp.bfloat16

ATTN_HEADS = 16
QK_DIM = 64
HEAD_W = 2 * QK_DIM
GRID_W = 64
N_MOD = 6
POOL_WINDOWS = (2, 4, 8, 16)
CONV_K = 31
HALO = 16
ROPE_BASE = 10000.0
EPS = 1e-6
QK_SCALE = QK_DIM ** -0.5 * math.log2(math.e)
BOUND_SLACK = 1.05
MAX_SCORE_BOUND = 60.0

LANES = 128
MXU_W = 256
VMEM_BYTES = 64 << 20


def _params(sem, vmem_mb):
    assert (vmem_mb << 20) <= VMEM_BYTES
    return pltpu.CompilerParams(dimension_semantics=sem, vmem_limit_bytes=vmem_mb << 20)


def _blk(n, pref):
    if n <= pref:
        return n
    b = pref
    while n % b:
        b //= 2
    return b


def _ada_kernel(cond_ref, w_ref, b_ref, o_ref):
    s = jax.nn.silu(cond_ref[...]).astype(BF16)
    o_ref[...] = jnp.dot(s, w_ref[...].astype(BF16), preferred_element_type=F32) + b_ref[...]


def _ada(cond8, w_ada, b_ada):
    depth, d, n = w_ada.shape
    bn = _blk(n, 512)
    return pl.pallas_call(
        _ada_kernel,
        grid=(depth, n // bn),
        in_specs=[
            pl.BlockSpec((8, d), lambda l, j: (0, 0)),
            pl.BlockSpec((None, d, bn), lambda l, j: (l, 0, j)),
            pl.BlockSpec((None, 1, bn), lambda l, j: (l, 0, j)),
        ],
        out_specs=pl.BlockSpec((None, 8, bn), lambda l, j: (l, 0, j)),
        out_shape=jax.ShapeDtypeStruct((depth, 8, n), F32),
        compiler_params=_params(("arbitrary", "arbitrary"), 40),
        name="ada_mod",
    )(cond8, w_ada, b_ada.reshape(depth, 1, n))


def _prenorm_kernel(x_ref, g_ref, sc_ref, sh_ref, o_ref):
    x = x_ref[...]
    y = x * lax.rsqrt(jnp.mean(x * x, axis=-1, keepdims=True) + EPS) * g_ref[...]
    o_ref[...] = (y * (1.0 + sc_ref[...]) + sh_ref[...]).astype(o_ref.dtype)


def _prenorm(x, g, scale, shift):
    L, d = x.shape
    bm = _blk(L, 512)
    row = pl.BlockSpec((1, d), lambda i: (0, 0))
    return pl.pallas_call(
        _prenorm_kernel,
        grid=(L // bm,),
        in_specs=[pl.BlockSpec((bm, d), lambda i: (i, 0)), row, row, row],
        out_specs=pl.BlockSpec((bm, d), lambda i: (i, 0)),
        out_shape=jax.ShapeDtypeStruct((L, d), BF16),
        compiler_params=_params(("arbitrary",), 40),
        name="prenorm",
    )(x, g.reshape(1, d), scale.reshape(1, d), shift.reshape(1, d))


def _mm_cast_kernel(a_ref, w_ref, o_ref, *, relu2):
    acc = jnp.dot(a_ref[...], w_ref[...], preferred_element_type=F32)
    if relu2:
        acc = jnp.square(jnp.maximum(acc, 0.0))
    o_ref[...] = acc.astype(o_ref.dtype)


def _mm_cast(a, w, col_off, n, *, relu2=False, bm=1024, bn=1024):
    M, K = a.shape
    bm, bn = _blk(M, bm), _blk(n, bn)
    assert col_off % bn == 0
    joff = col_off // bn
    return pl.pallas_call(
        functools.partial(_mm_cast_kernel, relu2=relu2),
        grid=(M // bm, n // bn),
        in_specs=[
            pl.BlockSpec((bm, K), lambda i, j: (i, 0)),
            pl.BlockSpec((K, bn), lambda i, j: (0, j + joff)),
        ],
        out_specs=pl.BlockSpec((bm, bn), lambda i, j: (i, j)),
        out_shape=jax.ShapeDtypeStruct((M, n), BF16),
        compiler_params=_params(("arbitrary", "arbitrary"), 48),
        name="proj_relu2" if relu2 else "proj_cast",
    )(a, w)


def _mm_vt_kernel(a_ref, w_ref, o_ref, *, tk):
    acc = jnp.dot(a_ref[...], w_ref[...], preferred_element_type=F32)
    for c in range(a_ref.shape[0] // tk):
        o_ref[c] = acc[c * tk:(c + 1) * tk, :].T.astype(o_ref.dtype)


def _mm_vt(a, w, col_off, n, tk, *, bm=1024, bn=512):
    M, K = a.shape
    bm, bn = _blk(M, bm), _blk(n, bn)
    assert bm % tk == 0 and col_off % bn == 0
    joff = col_off // bn
    return pl.pallas_call(
        functools.partial(_mm_vt_kernel, tk=tk),
        grid=(M // bm, n // bn),
        in_specs=[
            pl.BlockSpec((bm, K), lambda i, j: (i, 0)),
            pl.BlockSpec((K, bn), lambda i, j: (0, j + joff)),
        ],
        out_specs=pl.BlockSpec((bm // tk, bn, tk), lambda i, j: (i, j, 0)),
        out_shape=jax.ShapeDtypeStruct((M // tk, n, tk), BF16),
        compiler_params=_params(("arbitrary", "arbitrary"), 48),
        name="proj_vt",
    )(a, w)


def _mm_qk_kernel(a_ref, w_ref, gain_ref, gmat_ref, *rest, rope):
    if rope:
        cos_ref, sin_ref, o_ref = rest
    else:
        (o_ref,) = rest
    acc = jnp.dot(a_ref[...], w_ref[...], preferred_element_type=F32)
    bm, bn = acc.shape
    if rope:
        cos, sin = cos_ref[...], sin_ref[...]
        low_half = (lax.broadcasted_iota(jnp.int32, (bm, LANES), 1) & (QK_DIM // 2)) == 0
    for c in range(bn // MXU_W):
        t = acc[:, c * MXU_W:(c + 1) * MXU_W]
        ss = jnp.dot((t * t).astype(BF16), gmat_ref[...], preferred_element_type=F32)
        y = t * lax.rsqrt(ss * (1.0 / QK_DIM) + EPS) * gain_ref[:, c * MXU_W:(c + 1) * MXU_W]
        for h in range(MXU_W // LANES):
            yy = y[:, h * LANES:(h + 1) * LANES]
            if rope:
                partner = jnp.where(low_half, pltpu.roll(yy, LANES - QK_DIM // 2, 1), pltpu.roll(yy, QK_DIM // 2, 1))
                yy = yy * cos + partner * sin
            lo = c * MXU_W + h * LANES
            o_ref[:, lo:lo + LANES] = yy.astype(o_ref.dtype)


def _mm_qk(a, w, gain_row, gmat, cos=None, sin=None, *, bm=1024, bn=1024):
    M, K = a.shape
    n = gain_row.shape[1]
    bm, bn = _blk(M, bm), _blk(n, bn)
    rope = cos is not None
    in_specs = [
        pl.BlockSpec((bm, K), lambda i, j: (i, 0)),
        pl.BlockSpec((K, bn), lambda i, j: (0, j)),
        pl.BlockSpec((1, bn), lambda i, j: (0, j)),
        pl.BlockSpec((MXU_W, MXU_W), lambda i, j: (0, 0)),
    ]
    args = [a, w, gain_row, gmat]
    if rope:
        in_specs += [pl.BlockSpec((bm, LANES), lambda i, j: (i, 0))] * 2
        args += [cos, sin]
    return pl.pallas_call(
        functools.partial(_mm_qk_kernel, rope=rope),
        grid=(M // bm, n // bn),
        in_specs=in_specs,
        out_specs=pl.BlockSpec((bm, bn), lambda i, j: (i, j)),
        out_shape=jax.ShapeDtypeStruct((M, n), BF16),
        compiler_params=_params(("arbitrary", "arbitrary"), 52),
        name="proj_qk",
    )(*args)


def _mm_resid_kernel(*refs, n_lhs):
    a_refs, w_refs = refs[:n_lhs], refs[n_lhs:2 * n_lhs]
    x_ref, g_ref, o_ref, acc_ref = refs[2 * n_lhs:]
    k = pl.program_id(2)
    part = jnp.dot(a_refs[0][...], w_refs[0][...], preferred_element_type=F32)
    for a_ref, w_ref in zip(a_refs[1:], w_refs[1:]):
        part += jnp.dot(a_ref[...], w_ref[...], preferred_element_type=F32)

    @pl.when(k == 0)
    def _():
        acc_ref[...] = part

    @pl.when(k > 0)
    def _():
        acc_ref[...] += part

    @pl.when(k == pl.num_programs(2) - 1)
    def _():
        o_ref[...] = x_ref[...] + g_ref[...] * acc_ref[...]


def _mm_resid(lhs, w, x, gate, *, bk, bm=1024, bn=512):
    M, n = x.shape
    kw = lhs[0].shape[1]
    assert all(a.shape == (M, kw) for a in lhs) and w.shape == (kw * len(lhs), n) and kw % bk == 0
    bm, bn = _blk(M, bm), _blk(n, bn)
    ksteps = kw // bk
    in_specs = [pl.BlockSpec((bm, bk), lambda i, j, k: (i, k)) for _ in lhs]
    in_specs += [pl.BlockSpec((bk, bn), functools.partial(lambda i, j, k, s: (s * ksteps + k, j), s=s))
                 for s in range(len(lhs))]
    in_specs += [pl.BlockSpec((bm, bn), lambda i, j, k: (i, j)), pl.BlockSpec((1, bn), lambda i, j, k: (0, j))]
    return pl.pallas_call(
        functools.partial(_mm_resid_kernel, n_lhs=len(lhs)),
        grid=(M // bm, n // bn, ksteps),
        in_specs=in_specs,
        out_specs=pl.BlockSpec((bm, bn), lambda i, j, k: (i, j)),
        out_shape=jax.ShapeDtypeStruct((M, n), F32),
        scratch_shapes=[pltpu.VMEM((bm, bn), F32)],
        compiler_params=_params(("arbitrary", "arbitrary", "arbitrary"), 56),
        name="proj_resid",
    )(*lhs, *([w] * len(lhs)), x, gate.reshape(1, n))


_NT = (((1,), (1,)), ((), ()))


def _attn_kernel(*refs, tq, n_lat, lam_init):
    if n_lat:
        lamp_ref, subln_ref, q_ref, kc_ref, vc_ref, k_ref, v_ref, o_ref, qz_ref, m_ref, l_ref, acc_ref = refs[:12]
        tk = v_ref.shape[2]
    else:
        lamp_ref, subln_ref, q_ref, kc_ref, vc_ref, o_ref, qz_ref, m_ref, l_ref, acc_ref = refs[:10]
        tk = 0
    s_refs, p_refs, cm_refs = refs[-6:-4], refs[-4:-2], refs[-2:]
    tc = kc_ref.shape[0]
    n_chunks = 1 + n_lat

    _attn_fill_qz(q_ref, qz_ref, tq)
    m_ref[...] = jnp.full(m_ref.shape, -jnp.inf, F32)
    l_ref[...] = jnp.zeros(l_ref.shape, F32)
    acc_ref[...] = jnp.zeros(acc_ref.shape, F32)

    def scores(k_blk, slot):
        s = lax.dot_general(k_blk, qz_ref[...], _NT, preferred_element_type=F32)
        s_refs[slot][0:k_blk.shape[0], :] = s
        cm_refs[slot][...] = jnp.max(s, axis=0, keepdims=True)

    def values(c, slot):
        if isinstance(c, int) and c == 0:
            return jnp.dot(vc_ref[0], p_refs[slot][0:tc, :], preferred_element_type=F32)
        return jnp.dot(v_ref[c - 1], p_refs[slot][0:tk, :], preferred_element_type=F32)

    def iteration(c, slot, *, first, last):
        if not last:
            scores(k_ref[pl.ds(pl.multiple_of(c * tk, tk), tk), :], 1 - slot)
        pv = None if first else values(c - 1, 1 - slot)
        n = tc if first else tk
        m_old = m_ref[...]
        m_new = jnp.maximum(m_old, cm_refs[slot][...])
        alpha = jnp.exp2(m_old - m_new)
        p = jnp.exp2(s_refs[slot][0:n, :] - m_new)
        l_ref[...] = alpha * l_ref[...] + jnp.sum(p, axis=0, keepdims=True)
        p_refs[slot][0:n, :] = p.astype(BF16)
        m_ref[...] = m_new
        if not first:
            acc_ref[...] = alpha * (acc_ref[...] + pv)

    scores(kc_ref[...], 0)
    for c in range(min(2, n_chunks)):
        iteration(c, c % 2, first=c == 0, last=c == n_chunks - 1)
    steady = list(range(2, n_chunks - 1))
    if len(steady) >= 2:
        def pair(j, carry):
            c = 2 + 2 * j
            iteration(c, 0, first=False, last=False)
            iteration(c + 1, 1, first=False, last=False)
            return carry
        lax.fori_loop(0, len(steady) // 2, pair, 0)
    if len(steady) % 2:
        iteration(steady[-1], steady[-1] % 2, first=False, last=False)
    if n_chunks > 2:
        iteration(n_chunks - 1, (n_chunks - 1) % 2, first=False, last=True)
    acc = acc_ref[...] + values(n_chunks - 1, (n_chunks - 1) % 2)
    _attn_finish(acc, l_ref[...], lamp_ref, subln_ref, o_ref, tq, lam_init)


def _attn_finish(acc, denom, lamp_ref, subln_ref, o_ref, tq, lam_init):
    lp = lamp_ref[...]
    lam = (jnp.exp(jnp.sum(lp[0:1] * lp[1:2], axis=1, keepdims=True))
           - jnp.exp(jnp.sum(lp[2:3] * lp[3:4], axis=1, keepdims=True)) + lam_init)
    o = acc / denom
    d = o[:, :tq] - lam * o[:, tq:]
    d = d * lax.rsqrt(jnp.mean(d * d, axis=0, keepdims=True) + EPS)
    o_ref[...] = (d.T * subln_ref[...] * (1.0 - lam_init)).astype(o_ref.dtype)


def _attn_bounded_kernel(lamp_ref, subln_ref, q_ref, kc_ref, vc_ref, k_ref, v_ref, o_ref,
                         qz_ref, acc_ref, l_ref, p0_ref, p1_ref, *, tq, lam_init):
    n_lat, _, tk = v_ref.shape
    tc = kc_ref.shape[0]
    n_chunks = 1 + n_lat
    p_refs = (p0_ref, p1_ref)
    _attn_fill_qz(q_ref, qz_ref, tq)
    l_ref[...] = jnp.zeros(l_ref.shape, F32)

    def probs(k_blk, slot):
        n = k_blk.shape[0]
        e = jnp.exp2(lax.dot_general(k_blk, qz_ref[...], _NT, preferred_element_type=F32))
        l_ref[...] += jnp.sum(e.reshape(n // 8, 8, 2 * tq), axis=0)
        p_refs[slot][0:n, :] = e.astype(BF16)

    def iteration(c, slot, *, first, last):
        if not last:
            probs(k_ref[pl.ds(pl.multiple_of(c * tk, tk), tk), :], 1 - slot)
        if first:
            acc_ref[...] = jnp.dot(vc_ref[0], p_refs[slot][0:tc, :], preferred_element_type=F32)
        else:
            acc_ref[...] += jnp.dot(v_ref[c - 1], p_refs[slot][0:tk, :], preferred_element_type=F32)

    probs(kc_ref[...], 0)
    for c in range(n_chunks):
        iteration(c, c % 2, first=c == 0, last=c == n_chunks - 1)
    _attn_finish(acc_ref[...], jnp.sum(l_ref[...], axis=0, keepdims=True), lamp_ref, subln_ref, o_ref, tq, lam_init)


def _attn_fill_qz(q_ref, qz_ref, tq):
    q = q_ref[...]
    lane = lax.broadcasted_iota(jnp.int32, q.shape, 1)
    qz_ref[0:tq, :] = jnp.where(lane < QK_DIM, q, jnp.zeros_like(q))
    qz_ref[tq:2 * tq, :] = jnp.where(lane >= QK_DIM, q, jnp.zeros_like(q))


def _attention(lamp, subln, q_arr, q_col0, kc_arr, kc_col0, vct, k_arr=None, k_col0=0, vt=None, score_bound=None,
               *, lam_init, tq=512):
    Lq = q_arr.shape[0]
    C = kc_arr.shape[0]
    hw = vct.shape[1]
    assert vct.shape == (1, hw, C)
    heads = hw // HEAD_W
    tq = _blk(Lq, tq)
    n_lat, tk = (0, 0) if k_arr is None else (vt.shape[0], vt.shape[2])
    head_specs = [pl.BlockSpec((4, QK_DIM), lambda h, i: (0, 0)), pl.BlockSpec((1, HEAD_W), lambda h, i: (0, 0))]
    head_args = [lamp, subln.reshape(1, HEAD_W)]
    vc_spec = pl.BlockSpec((1, HEAD_W, C), lambda h, i: (0, h, 0))
    v_spec = pl.BlockSpec((n_lat, HEAD_W, tk), lambda h, i: (0, h, 0))
    in_specs = head_specs + [
        pl.BlockSpec((tq, HEAD_W), lambda h, i: (i, q_col0 + h)),
        pl.BlockSpec((C, HEAD_W), lambda h, i: (0, kc_col0 + h)),
        vc_spec,
    ]
    args = head_args + [q_arr, kc_arr, vct]
    if n_lat:
        assert k_arr.shape[0] == n_lat * tk
        in_specs += [pl.BlockSpec((k_arr.shape[0], HEAD_W), lambda h, i: (0, k_col0 + h)), v_spec]
        args += [k_arr, vt]
    tmax = max(C, tk)
    per_slot = lambda shape, dtype: [pltpu.VMEM(shape, dtype)] * 2
    common = dict(
        grid=(heads, Lq // tq),
        out_specs=pl.BlockSpec((tq, HEAD_W), lambda h, i: (i, h)),
        out_shape=jax.ShapeDtypeStruct((Lq, hw), BF16),
        compiler_params=_params(("arbitrary", "arbitrary"), 56),
    )
    general = pl.pallas_call(
        functools.partial(_attn_kernel, tq=tq, n_lat=n_lat, lam_init=lam_init),
        in_specs=in_specs,
        scratch_shapes=[
            pltpu.VMEM((2 * tq, HEAD_W), BF16),
            pltpu.VMEM((1, 2 * tq), F32),
            pltpu.VMEM((1, 2 * tq), F32),
            pltpu.VMEM((HEAD_W, 2 * tq), F32),
            *per_slot((tmax, 2 * tq), F32),
            *per_slot((tmax, 2 * tq), BF16),
            *per_slot((1, 2 * tq), F32),
        ],
        name="diff_attn",
        **common,
    )
    if score_bound is None:
        return general(*args)
    bounded = pl.pallas_call(
        functools.partial(_attn_bounded_kernel, tq=tq, lam_init=lam_init),
        in_specs=in_specs,
        scratch_shapes=[
            pltpu.VMEM((2 * tq, HEAD_W), BF16),
            pltpu.VMEM((HEAD_W, 2 * tq), F32),
            pltpu.VMEM((8, 2 * tq), F32),
            *per_slot((tmax, 2 * tq), BF16),
        ],
        name="diff_attn_bounded",
        **common,
    )
    return lax.cond(score_bound <= MAX_SCORE_BOUND, bounded, general, *args)


def _mixer_kernel(u_ref, up_ref, un_ref, pw_ref, ps_ref, dw_ref, db_ref, lg_ref, lb_ref, cw_ref, o_ref,
                  pool_ref, glu_ref, *, seq_len):
    bm = u_ref.shape[0]
    pw_cols = pool_ref.shape[1]
    grp = pw_cols // len(POOL_WINDOWS)
    i = pl.program_id(0)
    has_prev = (i > 0).astype(F32)
    has_next = (i < pl.num_programs(0) - 1).astype(F32)

    def fill(row0, src, valid):
        u = src.astype(F32)
        pool_ref[row0:row0 + src.shape[0], :] = u[:, :pw_cols] * valid
        a, gate = u[:, pw_cols:2 * pw_cols], u[:, 2 * pw_cols:]
        glu_ref[row0:row0 + src.shape[0], :] = a * jax.nn.sigmoid(gate) * valid

    fill(0, up_ref[...], has_prev)
    fill(HALO, u_ref[...], 1.0)
    fill(HALO + bm, un_ref[...], has_next)

    t = i * bm + lax.broadcasted_iota(jnp.int32, (bm, 1), 0)
    for g, w in enumerate(POOL_WINDOWS):
        cols = slice(g * grp, (g + 1) * grp)
        tot = pool_ref[HALO - w // 2:HALO - w // 2 + bm, cols]
        for dlt in range(-w // 2 + 1, w - w // 2):
            tot = tot + pool_ref[HALO + dlt:HALO + dlt + bm, cols]
        cnt = (jnp.clip(t + (w - w // 2), 0, seq_len) - jnp.clip(t - w // 2, 0, seq_len)).astype(F32)
        pooled = tot / cnt - pool_ref[HALO:HALO + bm, cols]
        y = jnp.dot(pooled.astype(BF16), pw_ref[g], preferred_element_type=F32) * ps_ref[:, cols]
        o_ref[:, cols] = y.astype(o_ref.dtype)

    first = HALO - CONV_K // 2
    y = None
    for r in range(8):
        z = None
        for m in range((first + CONV_K - 1) // 8 + 1):
            k = 8 * m + r - first
            if 0 <= k < CONV_K:
                term = glu_ref[8 * m:8 * m + bm + 8, :] * dw_ref[k:k + 1, :]
                z = term if z is None else z + term
        z = z[r:r + bm, :]
        y = z if y is None else y + z
    y = y + db_ref[...]
    mu = jnp.mean(y, axis=-1, keepdims=True)
    yc = y - mu
    var = jnp.mean(yc * yc, axis=-1, keepdims=True)
    z = yc * lax.rsqrt(var + EPS) * lg_ref[...] + lb_ref[...]
    z = jax.nn.silu(z)
    o_ref[:, pw_cols:] = jnp.dot(z.astype(BF16), cw_ref[...], preferred_element_type=F32).astype(o_ref.dtype)


def _mixer(upc, pool_w, pool_scale, conv_dw, conv_db, ln_g, ln_b, conv_pw, *, bm=256):
    L, w3 = upc.shape
    cw = w3 // 3
    bm = _blk(L, bm)
    nb, hb = L // bm, bm // HALO
    last_halo = L // HALO - 1
    dw_pad = jnp.zeros((32, cw), F32).at[:CONV_K].set(conv_dw)
    row = pl.BlockSpec((1, cw), lambda i: (0, 0))
    return pl.pallas_call(
        functools.partial(_mixer_kernel, seq_len=L),
        grid=(nb,),
        in_specs=[
            pl.BlockSpec((bm, w3), lambda i: (i, 0)),
            pl.BlockSpec((HALO, w3), lambda i: (jnp.maximum(i * hb - 1, 0), 0)),
            pl.BlockSpec((HALO, w3), lambda i: (jnp.minimum((i + 1) * hb, last_halo), 0)),
            pl.BlockSpec(pool_w.shape, lambda i: (0, 0, 0)),
            row,
            pl.BlockSpec((32, cw), lambda i: (0, 0)),
            row, row, row,
            pl.BlockSpec((cw, cw), lambda i: (0, 0)),
        ],
        out_specs=pl.BlockSpec((bm, 2 * cw), lambda i: (i, 0)),
        out_shape=jax.ShapeDtypeStruct((L, 2 * cw), BF16),
        scratch_shapes=[pltpu.VMEM((bm + 2 * HALO, cw), F32), pltpu.VMEM((bm + 2 * HALO, cw), F32)],
        compiler_params=_params(("arbitrary",), 40),
        name="pool_conv_mixer",
    )(upc, upc, upc, pool_w.astype(BF16), pool_scale.reshape(1, cw), dw_pad, conv_db.reshape(1, cw),
      ln_g.reshape(1, cw), ln_b.reshape(1, cw), conv_pw.astype(BF16))


def _rope_tables(n_tok):
    r = jnp.repeat(jnp.arange(n_tok // GRID_W), GRID_W).astype(F32)
    col = jnp.tile(jnp.arange(GRID_W), n_tok // GRID_W).astype(F32)
    n_freq = QK_DIM // 4
    inv = ROPE_BASE ** (-jnp.arange(n_freq, dtype=F32) / n_freq)
    ang = jnp.concatenate([r[:, None] * inv, col[:, None] * inv], axis=-1)
    cos, sin = jnp.cos(ang), jnp.sin(ang)
    reps = LANES // QK_DIM
    return jnp.tile(jnp.concatenate([cos, cos], -1), (1, reps)), jnp.tile(jnp.concatenate([-sin, sin], -1), (1, reps))


def _forward(x, c, ctx, c_ctx, w_ada, b_ada, norm1, norm2, w_in, w_out, q_norm, k_norm,
             lambda_q1, lambda_k1, lambda_q2, lambda_k2, subln, pool_w, pool_scale,
             conv_dw, conv_db, conv_ln_g, conv_ln_b, conv_pw, w_mlp1, w_mlp2, *, heads, tk):
    depth = w_in.shape[0]
    S, D = x.shape[1], x.shape[2]
    qw = heads * HEAD_W
    kv_off, pc_off = qw, 3 * qw
    pc_w = w_in.shape[2] - pc_off
    hidden = w_mlp1.shape[2]
    qblk = qw // HEAD_W

    xl = x[0]
    xc = ctx[0].astype(x.dtype)
    cos, sin = _rope_tables(S)
    gidx = jnp.arange(MXU_W) // QK_DIM
    gmat = (gidx[:, None] == gidx[None, :]).astype(BF16)

    cond8 = jnp.zeros((8, D), F32).at[0].set(c[0]).at[1].set(c_ctx)
    mods = _ada(cond8, w_ada, b_ada)

    for l in range(depth):
        last = l == depth - 1
        lam_init = 0.8 - 0.6 * math.exp(-0.3 * l)
        sh1, sc1, g1, sh2, sc2, g2 = jnp.split(mods[l, 0], N_MOD)
        csh1, csc1, cg1, csh2, csc2, cg2 = jnp.split(mods[l, 1], N_MOD)
        w_in_l, w_out_l = w_in[l].astype(BF16), w_out[l].astype(BF16)
        w1_l, w2_l = w_mlp1[l].astype(BF16), w_mlp2[l].astype(BF16)
        reps = qw // QK_DIM
        gain = jnp.concatenate([jnp.tile(q_norm[l], reps) * QK_SCALE,
                                jnp.tile(k_norm[l], reps)])[None, :]
        lamp = jnp.stack([lambda_q1[l], lambda_k1[l], lambda_q2[l], lambda_k2[l]]).astype(F32)
        mixer_w = (pool_w[l], pool_scale[l], conv_dw[l], conv_db[l], conv_ln_g[l], conv_ln_b[l], conv_pw[l])

        h = _prenorm(xl, norm1[l], sc1, sh1)
        hc = _prenorm(xc, norm1[l], csc1, csh1)
        qk = _mm_qk(h, w_in_l, gain, gmat, cos, sin)
        vt = _mm_vt(h, w_in_l, 2 * qw, qw, tk)
        upc = _mm_cast(h, w_in_l, pc_off, pc_w)
        qkc = _mm_qk(hc, w_in_l, gain, gmat)
        vct = _mm_vt(hc, w_in_l, 2 * qw, qw, xc.shape[0])

        score_bound = ((QK_DIM * QK_SCALE * BOUND_SLACK)
                       * jnp.max(jnp.abs(q_norm[l])) * jnp.max(jnp.abs(k_norm[l]))).astype(F32)
        attn = _attention(lamp, subln[l], qk, 0, qkc, qblk, vct, qk, qblk, vt, score_bound, lam_init=lam_init)
        pc = _mixer(upc, *mixer_w)
        x_new = _mm_resid([attn, pc], w_out_l, xl, g1, bk=qw)
        h2 = _prenorm(x_new, norm2[l], sc2, sh2)
        hid = _mm_cast(h2, w1_l, 0, hidden, relu2=True)
        x_new = _mm_resid([hid], w2_l, x_new, g2, bk=hidden, bm=512, bn=256)

        if not last:
            attn_c = _attention(lamp, subln[l], qkc, 0, qkc, qblk, vct, lam_init=lam_init)
            upcc = _mm_cast(hc, w_in_l, pc_off, pc_w)
            pcc = _mixer(upcc, *mixer_w)
            xc_new = _mm_resid([attn_c, pcc], w_out_l, xc, cg1, bk=qw)
            hc2 = _prenorm(xc_new, norm2[l], csc2, csh2)
            hidc = _mm_cast(hc2, w1_l, 0, hidden, relu2=True)
            xc = _mm_resid([hidc], w2_l, xc_new, cg2, bk=hidden, bm=512, bn=256)
        xl = x_new
    return xl[None]


def kernel(x, c, ctx, c_ctx, w_ada, b_ada, norm1, norm2, w_in, w_out, q_norm, k_norm, lambda_q1, lambda_k1,
           lambda_q2, lambda_k2, subln, pool_w, pool_scale, conv_dw, conv_db, conv_ln_g, conv_ln_b, conv_pw,
           w_mlp1, w_mlp2):
    return _forward(x, c, ctx, c_ctx, w_ada, b_ada, norm1, norm2, w_in, w_out, q_norm, k_norm,
                    lambda_q1, lambda_k1, lambda_q2, lambda_k2, subln, pool_w, pool_scale,
                    conv_dw, conv_db, conv_ln_g, conv_ln_b, conv_pw, w_mlp1, w_mlp2,
                    heads=ATTN_HEADS, tk=1024)
```

```python
import functools
import math

import jax
import jax.numpy as jnp
from jax import lax
from jax.experimental import pallas as pl
from jax.experimental.pallas import tpu as pltpu

F32 = jnp.float32
BF16 = jnp.bfloat16

ATTN_HEADS = 16
QK_DIM = 64
HEAD_W = 2 * QK_DIM
GRID_W = 64
N_MOD = 6
POOL_WINDOWS = (2, 4, 8, 16)
CONV_K = 31
HALO = 16
ROPE_BASE = 10000.0
EPS = 1e-6
QK_SCALE = QK_DIM ** -0.5 * math.log2(math.e)
BOUND_SLACK = 1.05
MAX_SCORE_BOUND = 60.0

LANES = 128
MXU_W = 256
VMEM_BYTES = 64 << 20
CAST_BLOCK_BYTES = 4 << 20


def _params(sem, vmem_mb):
    assert (vmem_mb << 20) <= VMEM_BYTES
    return pltpu.CompilerParams(dimension_semantics=sem, vmem_limit_bytes=vmem_mb << 20)


def _blk(n, pref):
    if n <= pref:
        return n
    b = pref
    while n % b:
        b //= 2
    return b


def _ada_kernel(cond_ref, w_ref, b_ref, o_ref):
    s = jax.nn.silu(cond_ref[...]).astype(BF16)
    o_ref[...] = jnp.dot(s, w_ref[...].astype(BF16), preferred_element_type=F32) + b_ref[...]


def _ada(cond8, w_ada, b_ada):
    depth, d, n = w_ada.shape
    bn = _blk(n, 512)
    return pl.pallas_call(
        _ada_kernel,
        grid=(depth, n // bn),
        in_specs=[
            pl.BlockSpec((8, d), lambda l, j: (0, 0)),
            pl.BlockSpec((None, d, bn), lambda l, j: (l, 0, j)),
            pl.BlockSpec((None, 1, bn), lambda l, j: (l, 0, j)),
        ],
        out_specs=pl.BlockSpec((None, 8, bn), lambda l, j: (l, 0, j)),
        out_shape=jax.ShapeDtypeStruct((depth, 8, n), F32),
        compiler_params=_params(("arbitrary", "arbitrary"), 40),
        name="ada_mod",
    )(cond8, w_ada, b_ada.reshape(depth, 1, n))


def _cast_kernel(w_ref, o_ref):
    o_ref[...] = w_ref[...].astype(o_ref.dtype)


def _cast_bf16(w, l):
    _, K, N = w.shape
    rows = _blk(K, 2 ** int(math.log2(CAST_BLOCK_BYTES // (4 * N))))
    return pl.pallas_call(
        _cast_kernel,
        grid=(K // rows,),
        in_specs=[pl.BlockSpec((None, rows, N), lambda i: (l, i, 0))],
        out_specs=pl.BlockSpec((rows, N), lambda i: (i, 0)),
        out_shape=jax.ShapeDtypeStruct((K, N), BF16),
        compiler_params=_params(("arbitrary",), 40),
        name="cast_bf16",
    )(w)


def _prenorm_kernel(x_ref, g_ref, sc_ref, sh_ref, o_ref):
    x = x_ref[...]
    y = x * lax.rsqrt(jnp.mean(x * x, axis=-1, keepdims=True) + EPS) * g_ref[...]
    o_ref[...] = (y * (1.0 + sc_ref[...]) + sh_ref[...]).astype(o_ref.dtype)


def _prenorm(x, g, scale, shift):
    L, d = x.shape
    bm = _blk(L, 512)
    row = pl.BlockSpec((1, d), lambda i: (0, 0))
    return pl.pallas_call(
        _prenorm_kernel,
        grid=(L // bm,),
        in_specs=[pl.BlockSpec((bm, d), lambda i: (i, 0)), row, row, row],
        out_specs=pl.BlockSpec((bm, d), lambda i: (i, 0)),
        out_shape=jax.ShapeDtypeStruct((L, d), BF16),
        compiler_params=_params(("arbitrary",), 40),
        name="prenorm",
    )(x, g.reshape(1, d), scale.reshape(1, d), shift.reshape(1, d))


def _mm_cast_kernel(a_ref, w_ref, o_ref, *, relu2):
    acc = jnp.dot(a_ref[...], w_ref[...], preferred_element_type=F32)
    if relu2:
        acc = jnp.square(jnp.maximum(acc, 0.0))
    o_ref[...] = acc.astype(o_ref.dtype)


def _mm_cast(a, w, col_off, n, *, relu2=False, bm=1024, bn=1024):
    M, K = a.shape
    bm, bn = _blk(M, bm), _blk(n, bn)
    assert col_off % bn == 0
    joff = col_off // bn
    return pl.pallas_call(
        functools.partial(_mm_cast_kernel, relu2=relu2),
        grid=(M // bm, n // bn),
        in_specs=[
            pl.BlockSpec((bm, K), lambda i, j: (i, 0)),
            pl.BlockSpec((K, bn), lambda i, j: (0, j + joff)),
        ],
        out_specs=pl.BlockSpec((bm, bn), lambda i, j: (i, j)),
        out_shape=jax.ShapeDtypeStruct((M, n), BF16),
        compiler_params=_params(("arbitrary", "arbitrary"), 48),
        name="proj_relu2" if relu2 else "proj_cast",
    )(a, w)


def _mm_vt_kernel(a_ref, w_ref, o_ref, *, tk):
    acc = jnp.dot(a_ref[...], w_ref[...], preferred_element_type=F32)
    for c in range(a_ref.shape[0] // tk):
        o_ref[c] = acc[c * tk:(c + 1) * tk, :].T.astype(o_ref.dtype)


def _mm_vt(a, w, col_off, n, tk, *, bm=1024, bn=512):
    M, K = a.shape
    bm, bn = _blk(M, bm), _blk(n, bn)
    assert bm % tk == 0 and col_off % bn == 0
    joff = col_off // bn
    return pl.pallas_call(
        functools.partial(_mm_vt_kernel, tk=tk),
        grid=(M // bm, n // bn),
        in_specs=[
            pl.BlockSpec((bm, K), lambda i, j: (i, 0)),
            pl.BlockSpec((K, bn), lambda i, j: (0, j + joff)),
        ],
        out_specs=pl.BlockSpec((bm // tk, bn, tk), lambda i, j: (i, j, 0)),
        out_shape=jax.ShapeDtypeStruct((M // tk, n, tk), BF16),
        compiler_params=_params(("arbitrary", "arbitrary"), 48),
        name="proj_vt",
    )(a, w)


def _mm_qk_kernel(a_ref, w_ref, gain_ref, gmat_ref, *rest, rope):
    if rope:
        cos_ref, sin_ref, o_ref = rest
    else:
        (o_ref,) = rest
    acc = jnp.dot(a_ref[...], w_ref[...], preferred_element_type=F32)
    bm, bn = acc.shape
    if rope:
        cos, sin = cos_ref[...], sin_ref[...]
        low_half = (lax.broadcasted_iota(jnp.int32, (bm, LANES), 1) & (QK_DIM // 2)) == 0
    for c in range(bn // MXU_W):
        t = acc[:, c * MXU_W:(c + 1) * MXU_W]
        ss = jnp.dot((t * t).astype(BF16), gmat_ref[...], preferred_element_type=F32)
        y = t * lax.rsqrt(ss * (1.0 / QK_DIM) + EPS) * gain_ref[:, c * MXU_W:(c + 1) * MXU_W]
        for h in range(MXU_W // LANES):
            yy = y[:, h * LANES:(h + 1) * LANES]
            if rope:
                partner = jnp.where(low_half, pltpu.roll(yy, LANES - QK_DIM // 2, 1), pltpu.roll(yy, QK_DIM // 2, 1))
                yy = yy * cos + partner * sin
            lo = c * MXU_W + h * LANES
            o_ref[:, lo:lo + LANES] = yy.astype(o_ref.dtype)


def _mm_qk(a, w, gain_row, gmat, cos=None, sin=None, *, bm=1024, bn=1024):
    M, K = a.shape
    n = gain_row.shape[1]
    bm, bn = _blk(M, bm), _blk(n, bn)
    rope = cos is not None
    in_specs = [
        pl.BlockSpec((bm, K), lambda i, j: (i, 0)),
        pl.BlockSpec((K, bn), lambda i, j: (0, j)),
        pl.BlockSpec((1, bn), lambda i, j: (0, j)),
        pl.BlockSpec((MXU_W, MXU_W), lambda i, j: (0, 0)),
    ]
    args = [a, w, gain_row, gmat]
    if rope:
        in_specs += [pl.BlockSpec((bm, LANES), lambda i, j: (i, 0))] * 2
        args += [cos, sin]
    return pl.pallas_call(
        functools.partial(_mm_qk_kernel, rope=rope),
        grid=(M // bm, n // bn),
        in_specs=in_specs,
        out_specs=pl.BlockSpec((bm, bn), lambda i, j: (i, j)),
        out_shape=jax.ShapeDtypeStruct((M, n), BF16),
        compiler_params=_params(("arbitrary", "arbitrary"), 52),
        name="proj_qk",
    )(*args)


def _mm_resid_kernel(*refs, n_lhs):
    a_refs, w_refs = refs[:n_lhs], refs[n_lhs:2 * n_lhs]
    x_ref, g_ref, o_ref, acc_ref = refs[2 * n_lhs:]
    k = pl.program_id(2)
    part = jnp.dot(a_refs[0][...], w_refs[0][...], preferred_element_type=F32)
    for a_ref, w_ref in zip(a_refs[1:], w_refs[1:]):
        part += jnp.dot(a_ref[...], w_ref[...], preferred_element_type=F32)

    @pl.when(k == 0)
    def _():
        acc_ref[...] = part

    @pl.when(k > 0)
    def _():
        acc_ref[...] += part

    @pl.when(k == pl.num_programs(2) - 1)
    def _():
        o_ref[...] = x_ref[...] + g_ref[...] * acc_ref[...]


def _mm_resid(lhs, w, x, gate, *, bk, bm=1024, bn=512):
    M, n = x.shape
    kw = lhs[0].shape[1]
    assert all(a.shape == (M, kw) for a in lhs) and w.shape == (kw * len(lhs), n) and kw % bk == 0
    bm, bn = _blk(M, bm), _blk(n, bn)
    ksteps = kw // bk
    in_specs = [pl.BlockSpec((bm, bk), lambda i, j, k: (i, k)) for _ in lhs]
    in_specs += [pl.BlockSpec((bk, bn), functools.partial(lambda i, j, k, s: (s * ksteps + k, j), s=s))
                 for s in range(len(lhs))]
    in_specs += [pl.BlockSpec((bm, bn), lambda i, j, k: (i, j)), pl.BlockSpec((1, bn), lambda i, j, k: (0, j))]
    return pl.pallas_call(
        functools.partial(_mm_resid_kernel, n_lhs=len(lhs)),
        grid=(M // bm, n // bn, ksteps),
        in_specs=in_specs,
        out_specs=pl.BlockSpec((bm, bn), lambda i, j, k: (i, j)),
        out_shape=jax.ShapeDtypeStruct((M, n), F32),
        scratch_shapes=[pltpu.VMEM((bm, bn), F32)],
        compiler_params=_params(("arbitrary", "arbitrary", "arbitrary"), 56),
        name="proj_resid",
    )(*lhs, *([w] * len(lhs)), x, gate.reshape(1, n))


_NT = (((1,), (1,)), ((), ()))


def _attn_kernel(*refs, tq, n_lat, lam_init):
    if n_lat:
        lamp_ref, subln_ref, q_ref, kc_ref, vc_ref, k_ref, v_ref, o_ref, qz_ref, m_ref, l_ref, acc_ref = refs[:12]
        tk = v_ref.shape[2]
    else:
        lamp_ref, subln_ref, q_ref, kc_ref, vc_ref, o_ref, qz_ref, m_ref, l_ref, acc_ref = refs[:10]
        tk = 0
    s_refs, p_refs, cm_refs = refs[-6:-4], refs[-4:-2], refs[-2:]
    tc = kc_ref.shape[0]
    n_chunks = 1 + n_lat

    _attn_fill_qz(q_ref, qz_ref, tq)
    m_ref[...] = jnp.full(m_ref.shape, -jnp.inf, F32)
    l_ref[...] = jnp.zeros(l_ref.shape, F32)
    acc_ref[...] = jnp.zeros(acc_ref.shape, F32)

    def scores(k_blk, slot):
        s = lax.dot_general(k_blk, qz_ref[...], _NT, preferred_element_type=F32)
        s_refs[slot][0:k_blk.shape[0], :] = s
        cm_refs[slot][...] = jnp.max(s, axis=0, keepdims=True)

    def values(c, slot):
        if isinstance(c, int) and c == 0:
            return jnp.dot(vc_ref[0], p_refs[slot][0:tc, :], preferred_element_type=F32)
        return jnp.dot(v_ref[c - 1], p_refs[slot][0:tk, :], preferred_element_type=F32)

    def iteration(c, slot, *, first, last):
        if not last:
            scores(k_ref[pl.ds(pl.multiple_of(c * tk, tk), tk), :], 1 - slot)
        pv = None if first else values(c - 1, 1 - slot)
        n = tc if first else tk
        m_old = m_ref[...]
        m_new = jnp.maximum(m_old, cm_refs[slot][...])
        alpha = jnp.exp2(m_old - m_new)
        p = jnp.exp2(s_refs[slot][0:n, :] - m_new)
        l_ref[...] = alpha * l_ref[...] + jnp.sum(p, axis=0, keepdims=True)
        p_refs[slot][0:n, :] = p.astype(BF16)
        m_ref[...] = m_new
        if not first:
            acc_ref[...] = alpha * (acc_ref[...] + pv)

    scores(kc_ref[...], 0)
    for c in range(min(2, n_chunks)):
        iteration(c, c % 2, first=c == 0, last=c == n_chunks - 1)
    steady = list(range(2, n_chunks - 1))
    if len(steady) >= 2:
        def pair(j, carry):
            c = 2 + 2 * j
            iteration(c, 0, first=False, last=False)
            iteration(c + 1, 1, first=False, last=False)
            return carry
        lax.fori_loop(0, len(steady) // 2, pair, 0)
    if len(steady) % 2:
        iteration(steady[-1], steady[-1] % 2, first=False, last=False)
    if n_chunks > 2:
        iteration(n_chunks - 1, (n_chunks - 1) % 2, first=False, last=True)
    acc = acc_ref[...] + values(n_chunks - 1, (n_chunks - 1) % 2)
    _attn_finish(acc, l_ref[...], lamp_ref, subln_ref, o_ref, tq, lam_init)


def _attn_finish(acc, denom, lamp_ref, subln_ref, o_ref, tq, lam_init):
    lp = lamp_ref[...]
    lam = (jnp.exp(jnp.sum(lp[0:1] * lp[1:2], axis=1, keepdims=True))
           - jnp.exp(jnp.sum(lp[2:3] * lp[3:4], axis=1, keepdims=True)) + lam_init)
    o = acc / denom
    d = o[:, :tq] - lam * o[:, tq:]
    d = d * lax.rsqrt(jnp.mean(d * d, axis=0, keepdims=True) + EPS)
    o_ref[...] = (d.T * subln_ref[...] * (1.0 - lam_init)).astype(o_ref.dtype)


def _attn_bounded_kernel(lamp_ref, subln_ref, q_ref, kc_ref, vc_ref, k_ref, v_ref, o_ref,
                         qz_ref, acc_ref, l_ref, p0_ref, p1_ref, *, tq, lam_init):
    n_lat, _, tk = v_ref.shape
    tc = kc_ref.shape[0]
    n_chunks = 1 + n_lat
    p_refs = (p0_ref, p1_ref)
    _attn_fill_qz(q_ref, qz_ref, tq)
    l_ref[...] = jnp.zeros(l_ref.shape, F32)

    def probs(k_blk, slot):
        n = k_blk.shape[0]
        e = jnp.exp2(lax.dot_general(k_blk, qz_ref[...], _NT, preferred_element_type=F32))
        l_ref[...] += jnp.sum(e.reshape(n // 8, 8, 2 * tq), axis=0)
        p_refs[slot][0:n, :] = e.astype(BF16)

    def iteration(c, slot, *, first, last):
        if not last:
            probs(k_ref[pl.ds(pl.multiple_of(c * tk, tk), tk), :], 1 - slot)
        if first:
            acc_ref[...] = jnp.dot(vc_ref[0], p_refs[slot][0:tc, :], preferred_element_type=F32)
        else:
            acc_ref[...] += jnp.dot(v_ref[c - 1], p_refs[slot][0:tk, :], preferred_element_type=F32)

    probs(kc_ref[...], 0)
    for c in range(n_chunks):
        iteration(c, c % 2, first=c == 0, last=c == n_chunks - 1)
    _attn_finish(acc_ref[...], jnp.sum(l_ref[...], axis=0, keepdims=True), lamp_ref, subln_ref, o_ref, tq, lam_init)


def _attn_fill_qz(q_ref, qz_ref, tq):
    q = q_ref[...]
    lane = lax.broadcasted_iota(jnp.int32, q.shape, 1)
    qz_ref[0:tq, :] = jnp.where(lane < QK_DIM, q, jnp.zeros_like(q))
    qz_ref[tq:2 * tq, :] = jnp.where(lane >= QK_DIM, q, jnp.zeros_like(q))


def _attention(lamp, subln, q_arr, q_col0, kc_arr, kc_col0, vct, k_arr=None, k_col0=0, vt=None, score_bound=None,
               *, lam_init, tq=1024):
    Lq = q_arr.shape[0]
    C = kc_arr.shape[0]
    hw = vct.shape[1]
    assert vct.shape == (1, hw, C)
    heads = hw // HEAD_W
    tq = _blk(Lq, tq)
    n_lat, tk = (0, 0) if k_arr is None else (vt.shape[0], vt.shape[2])
    head_specs = [pl.BlockSpec((4, QK_DIM), lambda h, i: (0, 0)), pl.BlockSpec((1, HEAD_W), lambda h, i: (0, 0))]
    head_args = [lamp, subln.reshape(1, HEAD_W)]
    vc_spec = pl.BlockSpec((1, HEAD_W, C), lambda h, i: (0, h, 0))
    v_spec = pl.BlockSpec((n_lat, HEAD_W, tk), lambda h, i: (0, h, 0))
    in_specs = head_specs + [
        pl.BlockSpec((tq, HEAD_W), lambda h, i: (i, q_col0 + h)),
        pl.BlockSpec((C, HEAD_W), lambda h, i: (0, kc_col0 + h)),
        vc_spec,
    ]
    args = head_args + [q_arr, kc_arr, vct]
    if n_lat:
        assert k_arr.shape[0] == n_lat * tk
        in_specs += [pl.BlockSpec((k_arr.shape[0], HEAD_W), lambda h, i: (0, k_col0 + h)), v_spec]
        args += [k_arr, vt]
    tmax = max(C, tk)
    per_slot = lambda shape, dtype: [pltpu.VMEM(shape, dtype)] * 2
    common = dict(
        grid=(heads, Lq // tq),
        out_specs=pl.BlockSpec((tq, HEAD_W), lambda h, i: (i, h)),
        out_shape=jax.ShapeDtypeStruct((Lq, hw), BF16),
        compiler_params=_params(("arbitrary", "arbitrary"), 56),
    )
    general = pl.pallas_call(
        functools.partial(_attn_kernel, tq=tq, n_lat=n_lat, lam_init=lam_init),
        in_specs=in_specs,
        scratch_shapes=[
            pltpu.VMEM((2 * tq, HEAD_W), BF16),
            pltpu.VMEM((1, 2 * tq), F32),
            pltpu.VMEM((1, 2 * tq), F32),
            pltpu.VMEM((HEAD_W, 2 * tq), F32),
            *per_slot((tmax, 2 * tq), F32),
            *per_slot((tmax, 2 * tq), BF16),
            *per_slot((1, 2 * tq), F32),
        ],
        name="diff_attn",
        **common,
    )
    if score_bound is None:
        return general(*args)
    bounded = pl.pallas_call(
        functools.partial(_attn_bounded_kernel, tq=tq, lam_init=lam_init),
        in_specs=in_specs,
        scratch_shapes=[
            pltpu.VMEM((2 * tq, HEAD_W), BF16),
            pltpu.VMEM((HEAD_W, 2 * tq), F32),
            pltpu.VMEM((8, 2 * tq), F32),
            *per_slot((tmax, 2 * tq), BF16),
        ],
        name="diff_attn_bounded",
        **common,
    )
    return lax.cond(score_bound <= MAX_SCORE_BOUND, bounded, general, *args)


def _mixer_kernel(u_ref, up_ref, un_ref, pw_ref, ps_ref, dw_ref, db_ref, lg_ref, lb_ref, cw_ref, o_ref,
                  pool_ref, glu_ref, *, seq_len):
    bm = u_ref.shape[0]
    pw_cols = pool_ref.shape[1]
    grp = pw_cols // len(POOL_WINDOWS)
    i = pl.program_id(0)
    has_prev = (i > 0).astype(F32)
    has_next = (i < pl.num_programs(0) - 1).astype(F32)

    def fill(row0, src, valid):
        u = src.astype(F32)
        pool_ref[row0:row0 + src.shape[0], :] = u[:, :pw_cols] * valid
        a, gate = u[:, pw_cols:2 * pw_cols], u[:, 2 * pw_cols:]
        glu_ref[row0:row0 + src.shape[0], :] = a * jax.nn.sigmoid(gate) * valid

    fill(0, up_ref[...], has_prev)
    fill(HALO, u_ref[...], 1.0)
    fill(HALO + bm, un_ref[...], has_next)

    t = i * bm + lax.broadcasted_iota(jnp.int32, (bm, 1), 0)
    for g, w in enumerate(POOL_WINDOWS):
        cols = slice(g * grp, (g + 1) * grp)
        tot = pool_ref[HALO - w // 2:HALO - w // 2 + bm, cols]
        for dlt in range(-w // 2 + 1, w - w // 2):
            tot = tot + pool_ref[HALO + dlt:HALO + dlt + bm, cols]
        cnt = (jnp.clip(t + (w - w // 2), 0, seq_len) - jnp.clip(t - w // 2, 0, seq_len)).astype(F32)
        pooled = tot / cnt - pool_ref[HALO:HALO + bm, cols]
        y = jnp.dot(pooled.astype(BF16), pw_ref[g], preferred_element_type=F32) * ps_ref[:, cols]
        o_ref[:, cols] = y.astype(o_ref.dtype)

    first = HALO - CONV_K // 2
    y = None
    for r in range(8):
        z = None
        for m in range((first + CONV_K - 1) // 8 + 1):
            k = 8 * m + r - first
            if 0 <= k < CONV_K:
                term = glu_ref[8 * m:8 * m + bm + 8, :] * dw_ref[k:k + 1, :]
                z = term if z is None else z + term
        z = z[r:r + bm, :]
        y = z if y is None else y + z
    y = y + db_ref[...]
    mu = jnp.mean(y, axis=-1, keepdims=True)
    yc = y - mu
    var = jnp.mean(yc * yc, axis=-1, keepdims=True)
    z = yc * lax.rsqrt(var + EPS) * lg_ref[...] + lb_ref[...]
    z = jax.nn.silu(z)
    o_ref[:, pw_cols:] = jnp.dot(z.astype(BF16), cw_ref[...], preferred_element_type=F32).astype(o_ref.dtype)


def _mixer(upc, pool_w, pool_scale, conv_dw, conv_db, ln_g, ln_b, conv_pw, *, bm=256):
    L, w3 = upc.shape
    cw = w3 // 3
    bm = _blk(L, bm)
    nb, hb = L // bm, bm // HALO
    last_halo = L // HALO - 1
    dw_pad = jnp.zeros((32, cw), F32).at[:CONV_K].set(conv_dw)
    row = pl.BlockSpec((1, cw), lambda i: (0, 0))
    return pl.pallas_call(
        functools.partial(_mixer_kernel, seq_len=L),
        grid=(nb,),
        in_specs=[
            pl.BlockSpec((bm, w3), lambda i: (i, 0)),
            pl.BlockSpec((HALO, w3), lambda i: (jnp.maximum(i * hb - 1, 0), 0)),
            pl.BlockSpec((HALO, w3), lambda i: (jnp.minimum((i + 1) * hb, last_halo), 0)),
            pl.BlockSpec(pool_w.shape, lambda i: (0, 0, 0)),
            row,
            pl.BlockSpec((32, cw), lambda i: (0, 0)),
            row, row, row,
            pl.BlockSpec((cw, cw), lambda i: (0, 0)),
        ],
        out_specs=pl.BlockSpec((bm, 2 * cw), lambda i: (i, 0)),
        out_shape=jax.ShapeDtypeStruct((L, 2 * cw), BF16),
        scratch_shapes=[pltpu.VMEM((bm + 2 * HALO, cw), F32), pltpu.VMEM((bm + 2 * HALO, cw), F32)],
        compiler_params=_params(("arbitrary",), 40),
        name="pool_conv_mixer",
    )(upc, upc, upc, pool_w.astype(BF16), pool_scale.reshape(1, cw), dw_pad, conv_db.reshape(1, cw),
      ln_g.reshape(1, cw), ln_b.reshape(1, cw), conv_pw.astype(BF16))


def _rope_tables(n_tok):
    r = jnp.repeat(jnp.arange(n_tok // GRID_W), GRID_W).astype(F32)
    col = jnp.tile(jnp.arange(GRID_W), n_tok // GRID_W).astype(F32)
    n_freq = QK_DIM // 4
    inv = ROPE_BASE ** (-jnp.arange(n_freq, dtype=F32) / n_freq)
    ang = jnp.concatenate([r[:, None] * inv, col[:, None] * inv], axis=-1)
    cos, sin = jnp.cos(ang), jnp.sin(ang)
    reps = LANES // QK_DIM
    return jnp.tile(jnp.concatenate([cos, cos], -1), (1, reps)), jnp.tile(jnp.concatenate([-sin, sin], -1), (1, reps))


def _forward(x, c, ctx, c_ctx, w_ada, b_ada, norm1, norm2, w_in, w_out, q_norm, k_norm,
             lambda_q1, lambda_k1, lambda_q2, lambda_k2, subln, pool_w, pool_scale,
             conv_dw, conv_db, conv_ln_g, conv_ln_b, conv_pw, w_mlp1, w_mlp2, *, heads, tk):
    depth = w_in.shape[0]
    S, D = x.shape[1], x.shape[2]
    qw = heads * HEAD_W
    kv_off, pc_off = qw, 3 * qw
    pc_w = w_in.shape[2] - pc_off
    hidden = w_mlp1.shape[2]
    qblk = qw // HEAD_W

    xl = x[0]
    xc = ctx[0].astype(x.dtype)
    cos, sin = _rope_tables(S)
    gidx = jnp.arange(MXU_W) // QK_DIM
    gmat = (gidx[:, None] == gidx[None, :]).astype(BF16)

    cond8 = jnp.zeros((8, D), F32).at[0].set(c[0]).at[1].set(c_ctx)
    mods = _ada(cond8, w_ada, b_ada)

    for l in range(depth):
        last = l == depth - 1
        lam_init = 0.8 - 0.6 * math.exp(-0.3 * l)
        sh1, sc1, g1, sh2, sc2, g2 = jnp.split(mods[l, 0], N_MOD)
        csh1, csc1, cg1, csh2, csc2, cg2 = jnp.split(mods[l, 1], N_MOD)
        w_in_l, w_out_l = _cast_bf16(w_in, l), _cast_bf16(w_out, l)
        w1_l, w2_l = _cast_bf16(w_mlp1, l), _cast_bf16(w_mlp2, l)
        reps = qw // QK_DIM
        gain = jnp.concatenate([jnp.tile(q_norm[l], reps) * QK_SCALE,
                                jnp.tile(k_norm[l], reps)])[None, :]
        lamp = jnp.stack([lambda_q1[l], lambda_k1[l], lambda_q2[l], lambda_k2[l]]).astype(F32)
        mixer_w = (pool_w[l], pool_scale[l], conv_dw[l], conv_db[l], conv_ln_g[l], conv_ln_b[l], conv_pw[l])

        h = _prenorm(xl, norm1[l], sc1, sh1)
        hc = _prenorm(xc, norm1[l], csc1, csh1)
        qk = _mm_qk(h, w_in_l, gain, gmat, cos, sin)
        vt = _mm_vt(h, w_in_l, 2 * qw, qw, tk)
        upc = _mm_cast(h, w_in_l, pc_off, pc_w)
        qkc = _mm_qk(hc, w_in_l, gain, gmat)
        vct = _mm_vt(hc, w_in_l, 2 * qw, qw, xc.shape[0])

        score_bound = ((QK_DIM * QK_SCALE * BOUND_SLACK)
                       * jnp.max(jnp.abs(q_norm[l])) * jnp.max(jnp.abs(k_norm[l]))).astype(F32)
        attn = _attention(lamp, subln[l], qk, 0, qkc, qblk, vct, qk, qblk, vt, score_bound, lam_init=lam_init)
        pc = _mixer(upc, *mixer_w)
        x_new = _mm_resid([attn, pc], w_out_l, xl, g1, bk=qw)
        h2 = _prenorm(x_new, norm2[l], sc2, sh2)
        hid = _mm_cast(h2, w1_l, 0, hidden, relu2=True)
        x_new = _mm_resid([hid], w2_l, x_new, g2, bk=hidden, bm=512, bn=256)

        if not last:
            attn_c = _attention(lamp, subln[l], qkc, 0, qkc, qblk, vct, lam_init=lam_init)
            upcc = _mm_cast(hc, w_in_l, pc_off, pc_w)
            pcc = _mixer(upcc, *mixer_w)
            xc_new = _mm_resid([attn_c, pcc], w_out_l, xc, cg1, bk=qw)
            hc2 = _prenorm(xc_new, norm2[l], csc2, csh2)
            hidc = _mm_cast(hc2, w1_l, 0, hidden, relu2=True)
            xc = _mm_resid([hidc], w2_l, xc_new, cg2, bk=hidden, bm=512, bn=256)
        xl = x_new
    return xl[None]


def kernel(x, c, ctx, c_ctx, w_ada, b_ada, norm1, norm2, w_in, w_out, q_norm, k_norm, lambda_q1, lambda_k1,
           lambda_q2, lambda_k2, subln, pool_w, pool_scale, conv_dw, conv_db, conv_ln_g, conv_ln_b, conv_pw,
           w_mlp1, w_mlp2):
    return _forward(x, c, ctx, c_ctx, w_ada, b_ada, norm1, norm2, w_in, w_out, q_norm, k_norm,
                    lambda_q1, lambda_k1, lambda_q2, lambda_k2, subln, pool_w, pool_scale,
                    conv_dw, conv_db, conv_ln_g, conv_ln_b, conv_pw, w_mlp1, w_mlp2,
                    heads=ATTN_HEADS, tk=1024)
```
